```python
import jax, jax.numpy as jnp
from jax import lax
import numpy as np

D_MODEL = 2048
BATCH = 8
SEQ = 4096
DEPTH = 2
DEC_BATCH = 32
DEC_SEQ = 16
PAST_LEN = 2048

CHUNK = 64
N_A_LAYERS = DEPTH // 2
N_B_LAYERS = DEPTH - N_A_LAYERS
HEAD_A = 64
H_A = D_MODEL // HEAD_A
DECAY_LORA = 96
AAA_LORA = 96
GATE_LORA = 256
GN_EPS = 64e-5
HEAD_DIM = 64
N_Q_HEADS = D_MODEL // HEAD_DIM
N_KV_HEADS = 4
Q_PER_KV = N_Q_HEADS // N_KV_HEADS
WINDOW = 128
W_CHUNKS = WINDOW // CHUNK
BAND = (W_CHUNKS + 1) * CHUNK
N_GROUPS = 4
EXPERTS_PER_GROUP = 8
N_EXPERTS = N_GROUPS * EXPERTS_PER_GROUP
TOP_K = 2
D_EXPERT = 512
MOE_BLOCK = 128
ALPHA = (2.0 * DEPTH) ** 0.25
BETA = (8.0 * DEPTH) ** -0.25
LN_EPS = 1e-5
NEG_INF = -1e30

kernel_name = "yoco_rwkv7_swa_sink_hmoe_step"


def layer_norm(x, g, b):
    xf = x.astype(jnp.float32)
    mu = xf.mean(-1, keepdims=True)
    var = jnp.square(xf - mu).mean(-1, keepdims=True)
    return ((xf - mu) * lax.rsqrt(var + LN_EPS) * g + b).astype(x.dtype)


def ada_mod(c, w, b):
    m = jax.nn.silu(c) @ w + b
    shift, scale, gate = jnp.split(m, 3, axis=-1)
    return shift[:, None], scale[:, None], gate[:, None]


def alibi_slopes():
    h = jnp.arange(N_Q_HEADS, dtype=jnp.float32) + 1.0
    return jnp.exp2(-8.0 * h / N_Q_HEADS).reshape(N_KV_HEADS, Q_PER_KV)


def rwkv7_time_mix(h, prev_row, s0, mu, wr, wk, wv, w0, w1, w2, a0, a1, a2, g1, g2,
                   kk_coef, ka_coef, r_k, lnx_g, lnx_b, wo):
    B, T, D = h.shape
    f32 = jnp.float32
    dx = jnp.concatenate([prev_row[:, None, :], h[:, :-1]], axis=1) - h
    xr, xw, xk, xv, xa, xg = (h + dx * mu[i] for i in range(6))
    r = xr @ wr
    k = xk @ wk
    v = xv @ wv
    log_w = -jax.nn.softplus(-(w0 + jnp.tanh(xw @ w1) @ w2).astype(f32)) - 0.5
    decay = jnp.exp(-jnp.exp(log_w))
    a = jax.nn.sigmoid((a0 + (xa @ a1) @ a2).astype(f32))
    g = jax.nn.sigmoid(xg @ g1) @ g2

    def heads(t):
        return t.astype(f32).reshape(B, T, H_A, HEAD_A)

    kk = heads(k * kk_coef)
    kk = kk * lax.rsqrt(jnp.maximum(jnp.sum(kk * kk, -1, keepdims=True), 1e-24))
    a_h = heads(a)
    k_h = heads(k) * (1.0 + (a_h - 1.0) * ka_coef.astype(f32).reshape(H_A, HEAD_A))
    r_h, v_h, w_h = heads(r), heads(v), heads(decay)

    def step(S, inp):
        r_t, w_t, k_t, v_t, kk_t, b_t = inp
        sa = jnp.einsum('bhvk,bhk->bhv', S, kk_t)
        S = S * w_t[:, :, None, :] - sa[..., None] * b_t[:, :, None, :] + v_t[..., None] * k_t[:, :, None, :]
        return S, jnp.einsum('bhvk,bhk->bhv', S, r_t)

    xs = tuple(jnp.swapaxes(t, 0, 1) for t in (r_h, w_h, k_h, v_h, kk, kk * a_h))
    s_fin, y = lax.scan(step, s0.astype(f32), xs)
    y = jnp.swapaxes(y, 0, 1)
    ym = y.mean(-1, keepdims=True)
    yv = jnp.square(y - ym).mean(-1, keepdims=True)
    yn = ((y - ym) * lax.rsqrt(yv + GN_EPS)).reshape(B, T, D) * lnx_g + lnx_b
    bonus = (jnp.sum(r_h * k_h * r_k.astype(f32), -1, keepdims=True) * v_h).reshape(B, T, D)
    out = ((yn + bonus) * g).astype(h.dtype) @ wo
    return out, s_fin.astype(s0.dtype), h[:, -1]


def alibi_sink_attend(q, k, v, q_pos, k_pos, slopes, sinks):
    s = jnp.einsum('bqgrd,bkgd->bgrqk', q, k).astype(jnp.float32) * (HEAD_DIM ** -0.5)
    dist = jnp.abs(q_pos[:, None] - k_pos[None, :]).astype(jnp.float32)
    s = s - slopes[:, :, None, None] * dist
    s = jnp.where(k_pos >= 0, s, NEG_INF)
    sink = sinks.astype(jnp.float32)[None, :, :, None]
    m = jnp.maximum(s.max(-1), sink)
    p = jnp.exp(s - m[..., None])
    denom = p.sum(-1) + jnp.exp(sink - m)
    p = (p / denom[..., None]).astype(v.dtype)
    return jnp.einsum('bgrqk,bkgd->bqgrd', p, v)


def band_attention_prompt(q, k, v, slopes, sinks):
    B, S = q.shape[:2]
    nc = S // CHUNK
    pad = W_CHUNKS * CHUNK

    def bands(t):
        tp = jnp.pad(t, ((0, 0), (pad, 0), (0, 0), (0, 0))).reshape(B, nc + W_CHUNKS, CHUNK, N_KV_HEADS, HEAD_DIM)
        tb = jnp.concatenate([tp[:, j:j + nc] for j in range(W_CHUNKS + 1)], axis=2)
        return jnp.moveaxis(tb, 1, 0)

    qc = jnp.moveaxis(q.reshape(B, nc, CHUNK, N_KV_HEADS, Q_PER_KV, HEAD_DIM), 1, 0)

    def one_chunk(args):
        qi, ki, vi, ci = args
        q_pos = ci * CHUNK + jnp.arange(CHUNK, dtype=jnp.int32)
        k_pos = (ci - W_CHUNKS) * CHUNK + jnp.arange(BAND, dtype=jnp.int32)
        return alibi_sink_attend(qi, ki, vi, q_pos, k_pos, slopes, sinks)

    o = lax.map(one_chunk, (qc, bands(k), bands(v), jnp.arange(nc, dtype=jnp.int32)))
    return jnp.moveaxis(o, 0, 1).reshape(B, S, N_Q_HEADS * HEAD_DIM)


def hier_moe(h, wg, bg, wr, br, w1, w3, w2):
    T, D = h.shape
    f32 = jnp.float32
    g_logits = (h @ wg + bg).astype(f32)
    g_sel = jnp.argmax(g_logits, -1).astype(jnp.int32)
    g_prob = jnp.take_along_axis(jax.nn.softmax(g_logits, -1), g_sel[:, None], -1)
    e_logits = (h @ wr + br).astype(f32).reshape(T, N_GROUPS, EXPERTS_PER_GROUP)
    e_sel = jnp.take_along_axis(e_logits, g_sel[:, None, None], axis=1)[:, 0]
    top_v, top_i = lax.top_k(e_sel, TOP_K)
    gates = jax.nn.softmax(top_v, -1) * g_prob
    expert = g_sel[:, None] * EXPERTS_PER_GROUP + top_i.astype(jnp.int32)
    n = T * TOP_K
    eid = expert.reshape(n)
    tok = jnp.repeat(jnp.arange(T, dtype=jnp.int32), TOP_K)
    gw = gates.reshape(n)
    order = jnp.argsort(eid)
    eid_s, tok_s, gw_s = eid[order], tok[order], gw[order]
    counts = jnp.zeros((N_EXPERTS,), jnp.int32).at[eid].add(1)
    padded = (counts + MOE_BLOCK - 1) // MOE_BLOCK * MOE_BLOCK
    start = jnp.cumsum(counts) - counts
    pend = jnp.cumsum(padded)
    pstart = pend - padded
    dest = pstart[eid_s] + (jnp.arange(n, dtype=jnp.int32) - start[eid_s])
    n_blocks = -(-n // MOE_BLOCK) + N_EXPERTS
    rows = n_blocks * MOE_BLOCK
    row_tok = jnp.full((rows,), T, jnp.int32).at[dest].set(tok_s)
    row_gate = jnp.zeros((rows,), f32).at[dest].set(gw_s)
    blk_start = jnp.arange(n_blocks, dtype=jnp.int32) * MOE_BLOCK
    blk_expert = jnp.minimum(jnp.searchsorted(pend, blk_start, side='right'), N_EXPERTS - 1)
    xb = jnp.concatenate([h, jnp.zeros((1, D), h.dtype)], 0)[row_tok].reshape(n_blocks, MOE_BLOCK, D)

    def expert_block(args):
        xblk, e = args
        return (jax.nn.silu(xblk @ w1[e]) * (xblk @ w3[e])) @ w2[e]

    yb = lax.map(expert_block, (xb, blk_expert)).reshape(rows, D)
    y = jax.ops.segment_sum(yb * row_gate[:, None].astype(yb.dtype), row_tok, num_segments=T + 1)
    return y[:T]


def shared_kv(x, c, kv_mod_w, kv_mod_b, w_kv):
    B, T, _ = x.shape
    shift, scale = jnp.split(jax.nn.silu(c) @ kv_mod_w + kv_mod_b, 2, axis=-1)
    hk = x * (1 + scale[:, None]) + shift[:, None]
    k, v = jnp.split(hk @ w_kv, 2, axis=-1)
    return k.reshape(B, T, N_KV_HEADS, HEAD_DIM), v.reshape(B, T, N_KV_HEADS, HEAD_DIM)


def run_trunk(x, c, shift0, wkv0, cache_k, cache_v, p):
    B, T, D = x.shape
    slopes = alibi_slopes()
    new_wkv, new_shift = [], []
    k_sh = v_sh = None
    for layer in range(DEPTH):
        shift, scale, gate = ada_mod(c, p['mod_w'][layer, 0], p['mod_b'][layer, 0])
        h = x * (1 + scale) + shift
        if layer < N_A_LAYERS:
            i = layer
            y, s_fin, last = rwkv7_time_mix(
                h, shift0[i], wkv0[i], p['rw_mu'][i], p['rw_wr'][i], p['rw_wk'][i], p['rw_wv'][i],
                p['rw_w0'][i], p['rw_w1'][i], p['rw_w2'][i], p['rw_a0'][i], p['rw_a1'][i], p['rw_a2'][i],
                p['rw_g1'][i], p['rw_g2'][i], p['rw_kk'][i], p['rw_ka'][i], p['rw_rk'][i],
                p['rw_lnx_g'][i], p['rw_lnx_b'][i], p['rw_wo'][i])
            new_wkv.append(s_fin)
            new_shift.append(last)
        else:
            j = layer - N_A_LAYERS
            q = (h @ p['at_wq'][j]).reshape(B, T, N_KV_HEADS, Q_PER_KV, HEAD_DIM)
            sinks = p['at_sinks'][j].reshape(N_KV_HEADS, Q_PER_KV)
            if cache_k is None:
                o = band_attention_prompt(q, k_sh, v_sh, slopes, sinks)
            else:
                k_all = jnp.concatenate([cache_k.astype(k_sh.dtype), k_sh], axis=1)
                v_all = jnp.concatenate([cache_v.astype(v_sh.dtype), v_sh], axis=1)
                q_pos = PAST_LEN + jnp.arange(T, dtype=jnp.int32)
                k_pos = PAST_LEN - WINDOW + jnp.arange(WINDOW + T, dtype=jnp.int32)
                o = alibi_sink_attend(q, k_all, v_all, q_pos, k_pos, slopes, sinks).reshape(B, T, D)
            y = o @ p['at_wo'][j]
        x = layer_norm(ALPHA * x + (1 + gate) * y, p['ln_g'][layer, 0], p['ln_b'][layer, 0])
        shift, scale, gate = ada_mod(c, p['mod_w'][layer, 1], p['mod_b'][layer, 1])
        h = x * (1 + scale) + shift
        y = hier_moe(h.reshape(B * T, D), p['moe_wg'][layer], p['moe_bg'][layer], p['moe_wr'][layer],
                     p['moe_br'][layer], p['moe_w1'][layer], p['moe_w3'][layer], p['moe_w2'][layer]).reshape(B, T, D)
        x = layer_norm(ALPHA * x + (1 + gate) * y, p['ln_g'][layer, 1], p['ln_b'][layer, 1])
        if layer == N_A_LAYERS - 1:
            k_sh, v_sh = shared_kv(x, c, p['kv_mod_w'], p['kv_mod_b'], p['w_kv'])
    if cache_k is None:
        k_out, v_out = k_sh[:, -WINDOW:], v_sh[:, -WINDOW:]
    else:
        k_out, v_out = k_sh, v_sh
    return x, jnp.stack(new_wkv), jnp.stack(new_shift), k_out, v_out


def setup_inputs(seed: int = 0) -> dict:
    key = jax.random.key(seed)
    ks = iter(jax.random.split(key, 64))

    def nrm(shape, scale):
        return jax.random.normal(next(ks), shape, jnp.float32) * scale

    def unif(shape, lo, hi):
        return jax.random.uniform(next(ks), shape, jnp.float32, lo, hi)

    D = D_MODEL
    sD = D ** -0.5
    KVW = N_KV_HEADS * HEAD_DIM
    return {
        'x_prompt': nrm((BATCH, SEQ, D), 1.0),
        'x_sample': nrm((DEC_BATCH, DEC_SEQ, D), 1.0),
        'c_prompt': nrm((BATCH, D), 1.0),
        'c_sample': nrm((DEC_BATCH, D), 1.0),
        'state_wkv': nrm((N_A_LAYERS, DEC_BATCH, H_A, HEAD_A, HEAD_A), 0.1),
        'state_shift': nrm((N_A_LAYERS, DEC_BATCH, D), 1.0),
        'cache_k': nrm((DEC_BATCH, WINDOW, N_KV_HEADS, HEAD_DIM), 1.0),
        'cache_v': nrm((DEC_BATCH, WINDOW, N_KV_HEADS, HEAD_DIM), 1.0),
        'mod_w': nrm((DEPTH, 2, D, 3 * D), 0.2 * sD),
        'mod_b': nrm((DEPTH, 2, 3 * D), 0.01),
        'ln_g': 1.0 + nrm((DEPTH, 2, D), 0.05),
        'ln_b': nrm((DEPTH, 2, D), 0.02),
        'rw_mu': unif((N_A_LAYERS, 6, D), 0.0, 1.0),
        'rw_wr': nrm((N_A_LAYERS, D, D), sD),
        'rw_wk': nrm((N_A_LAYERS, D, D), sD),
        'rw_wv': nrm((N_A_LAYERS, D, D), sD * BETA),
        'rw_w0': unif((N_A_LAYERS, D), -6.0, -1.0),
        'rw_w1': nrm((N_A_LAYERS, D, DECAY_LORA), sD),
        'rw_w2': nrm((N_A_LAYERS, DECAY_LORA, D), 0.5 * DECAY_LORA ** -0.5),
        'rw_a0': nrm((N_A_LAYERS, D), 0.1),
        'rw_a1': nrm((N_A_LAYERS, D, AAA_LORA), sD),
        'rw_a2': nrm((N_A_LAYERS, AAA_LORA, D), 0.5 * AAA_LORA ** -0.5),
        'rw_g1': nrm((N_A_LAYERS, D, GATE_LORA), sD),
        'rw_g2': nrm((N_A_LAYERS, GATE_LORA, D), GATE_LORA ** -0.5),
        'rw_kk': 0.85 + nrm((N_A_LAYERS, D), 0.05),
        'rw_ka': 1.0 + nrm((N_A_LAYERS, D), 0.05),
        'rw_rk': nrm((N_A_LAYERS, H_A, HEAD_A), 0.1),
        'rw_lnx_g': 1.0 + nrm((N_A_LAYERS, D), 0.05),
        'rw_lnx_b': nrm((N_A_LAYERS, D), 0.02),
        'rw_wo': nrm((N_A_LAYERS, D, D), sD * BETA),
        'kv_mod_w': nrm((D, 2 * D), 0.2 * sD),
        'kv_mod_b': nrm((2 * D,), 0.01),
        'w_kv': jnp.concatenate([nrm((D, KVW), sD), nrm((D, KVW), sD * BETA)], axis=1),
        'at_wq': nrm((N_B_LAYERS, D, N_Q_HEADS * HEAD_DIM), sD),
        'at_sinks': nrm((N_B_LAYERS, N_Q_HEADS), 0.5),
        'at_wo': nrm((N_B_LAYERS, N_Q_HEADS * HEAD_DIM, D), sD * BETA),
        'moe_wg': nrm((DEPTH, D, N_GROUPS), sD),
        'moe_bg': nrm((DEPTH, N_GROUPS), 0.01),
        'moe_wr': nrm((DEPTH, D, N_EXPERTS), sD),
        'moe_br': nrm((DEPTH, N_EXPERTS), 0.01),
        'moe_w1': nrm((DEPTH, N_EXPERTS, D, D_EXPERT), sD),
        'moe_w3': nrm((DEPTH, N_EXPERTS, D, D_EXPERT), sD),
        'moe_w2': nrm((DEPTH, N_EXPERTS, D_EXPERT, D), D_EXPERT ** -0.5 * BETA),
    }


def reference(x_prompt, x_sample, c_prompt, c_sample, state_wkv, state_shift, cache_k, cache_v,
              mod_w, mod_b, ln_g, ln_b, rw_mu, rw_wr, rw_wk, rw_wv, rw_w0, rw_w1, rw_w2,
              rw_a0, rw_a1, rw_a2, rw_g1, rw_g2, rw_kk, rw_ka, rw_rk, rw_lnx_g, rw_lnx_b, rw_wo,
              kv_mod_w, kv_mod_b, w_kv, at_wq, at_sinks, at_wo,
              moe_wg, moe_bg, moe_wr, moe_br, moe_w1, moe_w3, moe_w2):
    params = dict(mod_w=mod_w, mod_b=mod_b, ln_g=ln_g, ln_b=ln_b, rw_mu=rw_mu, rw_wr=rw_wr, rw_wk=rw_wk,
                  rw_wv=rw_wv, rw_w0=rw_w0, rw_w1=rw_w1, rw_w2=rw_w2, rw_a0=rw_a0, rw_a1=rw_a1, rw_a2=rw_a2,
                  rw_g1=rw_g1, rw_g2=rw_g2, rw_kk=rw_kk, rw_ka=rw_ka, rw_rk=rw_rk, rw_lnx_g=rw_lnx_g,
                  rw_lnx_b=rw_lnx_b, rw_wo=rw_wo, kv_mod_w=kv_mod_w, kv_mod_b=kv_mod_b, w_kv=w_kv,
                  at_wq=at_wq, at_sinks=at_sinks, at_wo=at_wo, moe_wg=moe_wg, moe_bg=moe_bg,
                  moe_wr=moe_wr, moe_br=moe_br, moe_w1=moe_w1, moe_w3=moe_w3, moe_w2=moe_w2)
    B = x_prompt.shape[0]
    zero_shift = jnp.zeros((N_A_LAYERS, B, D_MODEL), x_prompt.dtype)
    zero_wkv = jnp.zeros((N_A_LAYERS, B, H_A, HEAD_A, HEAD_A), x_prompt.dtype)
    y_prompt, wkv_p, shift_p, k_p, v_p = run_trunk(x_prompt, c_prompt, zero_shift, zero_wkv, None, None, params)
    y_sample, wkv_s, shift_s, k_s, v_s = run_trunk(x_sample, c_sample, state_shift, state_wkv, cache_k, cache_v, params)
    return (y_prompt, y_sample, wkv_p, shift_p, k_p, v_p, wkv_s, shift_s, k_s, v_s)
```

```python
import functools

import jax
import jax.numpy as jnp
from jax import lax
from jax.experimental import pallas as pl
from jax.experimental.pallas import tpu as pltpu

F32 = jnp.float32
BF16 = jnp.bfloat16

HEAD = 64
N_KV_HEADS = 4
WINDOW = 128
CHUNK = 64
N_GROUPS = 4
EXPERTS_PER_GROUP = 8
N_EXPERTS = N_GROUPS * EXPERTS_PER_GROUP
TOP_K = 2
DEPTH = 2
ALPHA = (2.0 * DEPTH) ** 0.25
LN_EPS = 1e-5
GN_EPS = 64e-5
NEG_INF = -1e30

LANES = 128
PAIR = LANES // HEAD

PROMPT_ROWS = 512
MOE_ROWS = 256
MODS_TN = 1024
VMEM_LIMIT = 56 * 1024 * 1024

NN = (((1,), (0,)), ((), ()))
NT = (((1,), (1,)), ((), ()))
TN = (((0,), (0,)), ((), ()))


def _mm(a, b, dims=NN, passes=1):
    if passes == 6:
        return lax.dot_general(a, b, dims, precision=lax.Precision.HIGHEST, preferred_element_type=F32)
    ah = a.astype(BF16)
    bh = b.astype(BF16)
    out = lax.dot_general(ah, bh, dims, preferred_element_type=F32)
    if passes == 3:
        al = (a - ah.astype(F32)).astype(BF16)
        bl = (b - bh.astype(F32)).astype(BF16)
        out = out + lax.dot_general(ah, bl, dims, preferred_element_type=F32)
        out = out + lax.dot_general(al, bh, dims, preferred_element_type=F32)
    return out


def _sigmoid(x):
    return 1.0 / (1.0 + jnp.exp(-x))


def _layer_norm(v, g, b):
    mu = jnp.mean(v, axis=-1, keepdims=True)
    d = v - mu
    var = jnp.mean(d * d, axis=-1, keepdims=True)
    return d * lax.rsqrt(var + LN_EPS) * g + b


def _params(*sem):
    return pltpu.CompilerParams(dimension_semantics=sem, vmem_limit_bytes=VMEM_LIMIT)


def _mods_kernel(c_ref, w_ref, b_ref, o_ref):
    c = c_ref[...]
    a = (c * _sigmoid(c)).astype(BF16)
    o_ref[0] = jnp.dot(a, w_ref[0].astype(BF16), preferred_element_type=F32) + b_ref[0]


def _mods(c, w, b):
    g, d, n = w.shape
    m = c.shape[0]
    tn = min(MODS_TN, n)
    assert n % tn == 0
    return pl.pallas_call(
        _mods_kernel,
        out_shape=jax.ShapeDtypeStruct((g, m, n), F32),
        grid=(g, n // tn),
        in_specs=[
            pl.BlockSpec((m, d), lambda i, j: (0, 0)),
            pl.BlockSpec((1, d, tn), lambda i, j: (i, 0, j)),
            pl.BlockSpec((1, 1, tn), lambda i, j: (i, 0, j)),
        ],
        out_specs=pl.BlockSpec((1, m, tn), lambda i, j: (i, 0, j)),
        compiler_params=_params("arbitrary", "arbitrary"),
        name="mods",
    )(c, w, b)


def _modulate(x_ref, sc_ref, sh_ref):
    nb, tt, d = x_ref.shape
    h = x_ref[...] * (1.0 + sc_ref[...]) + sh_ref[...]
    return h.reshape(nb * tt, d)


def _token_shift(h, prev_scr, prev0_ref, tt, t_axis):
    rows, d = h.shape
    nb = rows // tt

    @pl.when(pl.program_id(t_axis) == 0)
    def _():
        prev_scr[...] = prev0_ref[...]

    prev = jnp.broadcast_to(prev_scr[...], (nb, tt, d)).reshape(rows, d)
    rolled = pltpu.roll(h, 1, 0)
    row = lax.broadcasted_iota(jnp.int32, (rows, 1), 0)
    first = (row & (tt - 1)) == 0
    hp = jnp.where(first, prev, rolled)
    prev_scr[...] = h.reshape(nb, tt, d)[:, tt - 1:tt, :]
    return hp


def _rkv_kernel(x_ref, sc_ref, sh_ref, prev0_ref, mu_ref, w_ref, o_ref, prev_scr):
    nb, tt, d = x_ref.shape
    h = _modulate(x_ref, sc_ref, sh_ref)
    hp = _token_shift(h, prev_scr, prev0_ref, tt, 2)
    xin = h + (hp - h) * mu_ref[0]
    out = jnp.dot(xin.astype(BF16), w_ref[0], preferred_element_type=F32)
    o_ref[0] = out.reshape(nb, tt, out.shape[-1])


def _rkv(x, sc, sh, prev0, mu3, w3, nb, tt):
    NB, TT, D = x.shape
    seq = lambda j, b, t: (b, t, 0)
    per = lambda j, b, t: (b, 0, 0)
    return pl.pallas_call(
        _rkv_kernel,
        out_shape=jax.ShapeDtypeStruct((3, NB, TT, D), F32),
        grid=(3, NB // nb, TT // tt),
        in_specs=[
            pl.BlockSpec((nb, tt, D), seq),
            pl.BlockSpec((nb, 1, D), per),
            pl.BlockSpec((nb, 1, D), per),
            pl.BlockSpec((nb, 1, D), per),
            pl.BlockSpec((1, 1, D), lambda j, b, t: (j, 0, 0)),
            pl.BlockSpec((1, D, D), lambda j, b, t: (j, 0, 0)),
        ],
        out_specs=pl.BlockSpec((1, nb, tt, D), lambda j, b, t: (j, b, t, 0)),
        scratch_shapes=[pltpu.VMEM((nb, 1, D), F32)],
        compiler_params=_params("arbitrary", "arbitrary", "arbitrary"),
        name="tmix_rkv",
    )(x, sc, sh, prev0, mu3, w3)


def _lora_kernel(x_ref, sc_ref, sh_ref, prev0_ref, mu_ref, w1_ref, w2_ref, w0_ref, a1_ref, a2_ref, a0_ref,
                 g1_ref, g2_ref, lw_ref, a_ref, g_ref, last_ref, prev_scr):
    nb, tt, d = x_ref.shape
    h = _modulate(x_ref, sc_ref, sh_ref)
    hp = _token_shift(h, prev_scr, prev0_ref, tt, 1)
    dx = hp - h
    last_ref[...] = h.reshape(nb, tt, d)[:, tt - 1:tt, :]

    xw = (h + dx * mu_ref[0:1, :]).astype(BF16)
    zw = jnp.tanh(jnp.dot(xw, w1_ref[...], preferred_element_type=F32))
    zw = jnp.dot(zw.astype(BF16), w2_ref[...], preferred_element_type=F32) + w0_ref[...]
    sp = jnp.maximum(-zw, 0.0) + jnp.log1p(jnp.exp(-jnp.abs(zw)))
    lw_ref[...] = (-jnp.exp(-sp - 0.5)).reshape(nb, tt, d)

    xa = (h + dx * mu_ref[1:2, :]).astype(BF16)
    za = jnp.dot(xa, a1_ref[...], preferred_element_type=F32)
    za = jnp.dot(za.astype(BF16), a2_ref[...], preferred_element_type=F32) + a0_ref[...]
    a_ref[...] = _sigmoid(za).reshape(nb, tt, d)

    xg = (h + dx * mu_ref[2:3, :]).astype(BF16)
    zg = _sigmoid(jnp.dot(xg, g1_ref[...], preferred_element_type=F32))
    g_ref[...] = jnp.dot(zg.astype(BF16), g2_ref[...], preferred_element_type=F32).reshape(nb, tt, d)


def _lora(x, sc, sh, prev0, mu3, w1, w2, w0, a1, a2, a0, g1, g2, nb, tt):
    NB, TT, D = x.shape
    seq = lambda b, t: (b, t, 0)
    per = lambda b, t: (b, 0, 0)
    full = lambda a: pl.BlockSpec(a.shape, lambda b, t: (0,) * a.ndim)
    big = jax.ShapeDtypeStruct((NB, TT, D), F32)
    return pl.pallas_call(
        _lora_kernel,
        out_shape=(big, big, big, jax.ShapeDtypeStruct((NB, 1, D), F32)),
        grid=(NB // nb, TT // tt),
        in_specs=[
            pl.BlockSpec((nb, tt, D), seq),
            pl.BlockSpec((nb, 1, D), per),
            pl.BlockSpec((nb, 1, D), per),
            pl.BlockSpec((nb, 1, D), per),
            full(mu3), full(w1), full(w2), full(w0), full(a1), full(a2), full(a0), full(g1), full(g2),
        ],
        out_specs=(pl.BlockSpec((nb, tt, D), seq),) * 3 + (pl.BlockSpec((nb, 1, D), per),),
        scratch_shapes=[pltpu.VMEM((nb, 1, D), F32)],
        compiler_params=_params("arbitrary", "arbitrary"),
        name="tmix_lora",
    )(x, sc, sh, prev0, mu3, w1, w2, w0, a1, a2, a0, g1, g2)


def _wkv_kernel(r_ref, k_ref, v_ref, lw_ref, a_ref, g_ref, kkc_ref, kac_ref, rk_ref, lng_ref, lnb_ref, s0_ref,
                z_ref, sfin_ref, s_scr, *, passes):
    L = r_ref.shape[2]
    n_heads = s_scr.shape[0]
    c = pl.program_id(1)

    @pl.when(c == 0)
    def _():
        s_scr[...] = jnp.zeros(s_scr.shape, F32)
        for h in range(n_heads):
            o = (h % PAIR) * HEAD
            s_scr[h, o:o + HEAD, o:o + HEAD] = s0_ref[0, h]

    lane = lax.broadcasted_iota(jnp.int32, (1, LANES), 1)
    ri = lax.broadcasted_iota(jnp.int32, (L, L), 0)
    ci = lax.broadcasted_iota(jnp.int32, (L, L), 1)
    strict = ri > ci
    incl = ri >= ci
    tril = jnp.where(incl, 1.0, 0.0).astype(F32)
    eye = jnp.where(ri == ci, 1.0, 0.0).astype(F32)
    bi = lax.broadcasted_iota(jnp.int32, (LANES, LANES), 0) // HEAD
    bj = lax.broadcasted_iota(jnp.int32, (LANES, LANES), 1) // HEAD
    head_ones = jnp.where(bi == bj, 1.0, 0.0).astype(F32)

    def pair_body(p, carry):
        sl = pl.ds(pl.multiple_of(p * LANES, LANES), LANES)
        rp = r_ref[0, 0, :, sl]
        kp = k_ref[0, 0, :, sl]
        vp = v_ref[0, 0, :, sl]
        lwp = lw_ref[0, :, sl]
        ap = a_ref[0, :, sl]
        cum = _mm(tril, lwp, passes=6)
        c_incl = jnp.exp(cum)
        c_excl = jnp.exp(cum - lwp)
        c_inv = jnp.exp(-cum)
        kkp = kp * kkc_ref[:, sl]
        n2 = _mm(kkp * kkp, head_ones, passes=6)
        kkp = kkp * lax.rsqrt(jnp.maximum(n2, 1e-24))
        khp = kp * (1.0 + (ap - 1.0) * kac_ref[:, sl])
        bp = kkp * ap
        kt = kkp * c_excl
        rt = rp * c_incl
        kti = khp * c_inv
        bti = bp * c_inv
        lhs = jnp.concatenate([kt, rt], axis=0)
        c_last = c_incl[L - 1:L, :]
        ys = []
        for hh in range(PAIR):
            m = (lane >= HEAD) if hh else (lane < HEAD)
            lhs_m = jnp.where(m, lhs, 0.0)
            gk = _mm(lhs_m, kti, NT, passes)
            gb = _mm(lhs_m, bti, NT, passes)
            a_kk = jnp.where(strict, gk[:L], 0.0)
            a_rk = jnp.where(incl, gk[L:], 0.0)
            a_kb = jnp.where(strict, gb[:L], 0.0)
            a_rb = jnp.where(incl, gb[L:], 0.0)
            s0 = s_scr[PAIR * p + hh]
            p0 = _mm(lhs, s0, NT, passes)
            pw = -a_kb
            t_inv = eye + pw
            n = 1
            while n * 2 < L:
                pw = _mm(pw, pw, NN, passes)
                t_inv = t_inv + _mm(t_inv, pw, NN, passes)
                n *= 2
            x = p0[:L] + _mm(a_kk, vp, NN, passes)
            u = _mm(t_inv, x, NN, passes)
            y = p0[L:] + _mm(a_rk, vp, NN, passes) - _mm(a_rb, u, NN, passes)
            vu = jnp.where(m, jnp.concatenate([vp, u], axis=0), 0.0)
            kb = jnp.where(m, jnp.concatenate([kti, -bti], axis=0), 0.0)
            ds = _mm(vu, kb, TN, passes)
            s_scr[PAIR * p + hh] = (s0 + ds) * c_last
            ys.append(y)
        y = jnp.where(lane < HEAD, ys[0], ys[1])
        mean = _mm(y, head_ones, passes=6) * (1.0 / HEAD)
        dlt = y - mean
        var = _mm(dlt * dlt, head_ones, passes=6) * (1.0 / HEAD)
        yn = dlt * lax.rsqrt(var + GN_EPS) * lng_ref[:, sl] + lnb_ref[:, sl]
        bonus = _mm(rp * khp * rk_ref[:, sl], head_ones, passes=6) * vp
        z_ref[0, :, sl] = ((yn + bonus) * g_ref[0, :, sl]).astype(z_ref.dtype)
        return carry

    lax.fori_loop(0, n_heads // PAIR, pair_body, 0)

    @pl.when(c == pl.num_programs(1) - 1)
    def _():
        for h in range(n_heads):
            o = (h % PAIR) * HEAD
            sfin_ref[0, h] = s_scr[h, o:o + HEAD, o:o + HEAD]


def _wkv(rkv, lw, a, g, kkc, kac, rk, lng, lnb, s0, L, passes):
    _, NB, TT, D = rkv.shape
    H = D // HEAD
    seq = lambda b, t: (b, t, 0)
    vec = pl.BlockSpec((1, D), lambda b, t: (0, 0))
    st = pl.BlockSpec((1, H, HEAD, HEAD), lambda b, t: (b, 0, 0, 0))
    return pl.pallas_call(
        functools.partial(_wkv_kernel, passes=passes),
        out_shape=(jax.ShapeDtypeStruct((NB, TT, D), BF16), jax.ShapeDtypeStruct((NB, H, HEAD, HEAD), F32)),
        grid=(NB, TT // L),
        in_specs=[
            pl.BlockSpec((1, 1, L, D), lambda b, t: (0, b, t, 0)),
            pl.BlockSpec((1, 1, L, D), lambda b, t: (1, b, t, 0)),
            pl.BlockSpec((1, 1, L, D), lambda b, t: (2, b, t, 0)),
            pl.BlockSpec((1, L, D), seq),
            pl.BlockSpec((1, L, D), seq),
            pl.BlockSpec((1, L, D), seq),
            vec, vec, vec, vec, vec, st,
        ],
        out_specs=(pl.BlockSpec((1, L, D), seq), st),
        scratch_shapes=[pltpu.VMEM((H, LANES, LANES), F32)],
        compiler_params=_params("arbitrary", "arbitrary"),
        name="wkv",
    )(rkv, rkv, rkv, lw, a, g, kkc, kac, rk, lng, lnb, s0)


def _outproj_ln_kernel(a_ref, x_ref, gate_ref, lng_ref, lnb_ref, w_ref, o_ref):
    nb, tt, d = x_ref.shape
    a = a_ref[...].reshape(nb * tt, a_ref.shape[-1])
    y = jnp.dot(a, w_ref[...], preferred_element_type=F32).reshape(nb, tt, d)
    v = ALPHA * x_ref[...] + (1.0 + gate_ref[...]) * y
    o_ref[...] = _layer_norm(v, lng_ref[...], lnb_ref[...])


def _outproj_ln(a, x, gate, lng, lnb, w, nb, tt):
    NB, TT, D = x.shape
    seq = lambda b, t: (b, t, 0)
    per = lambda b, t: (b, 0, 0)
    vec = pl.BlockSpec((1, 1, D), lambda b, t: (0, 0, 0))
    return pl.pallas_call(
        _outproj_ln_kernel,
        out_shape=jax.ShapeDtypeStruct((NB, TT, D), F32),
        grid=(NB // nb, TT // tt),
        in_specs=[
            pl.BlockSpec((nb, tt, a.shape[-1]), seq),
            pl.BlockSpec((nb, tt, D), seq),
            pl.BlockSpec((nb, 1, D), per),
            vec, vec,
            pl.BlockSpec(w.shape, lambda b, t: (0, 0)),
        ],
        out_specs=pl.BlockSpec((nb, tt, D), seq),
        compiler_params=_params("arbitrary", "arbitrary"),
        name="outproj_ln",
    )(a, x, gate, lng, lnb, w)


def _resid_ln_kernel(x_ref, y_ref, gate_ref, lng_ref, lnb_ref, o_ref):
    y = y_ref[0] + y_ref[1]
    v = ALPHA * x_ref[...] + (1.0 + gate_ref[...]) * y
    o_ref[...] = _layer_norm(v, lng_ref[...], lnb_ref[...])


def _resid_ln(x, y2, gate, lng, lnb, nb, tt):
    NB, TT, D = x.shape
    seq = lambda b, t: (b, t, 0)
    per = lambda b, t: (b, 0, 0)
    vec = pl.BlockSpec((1, 1, D), lambda b, t: (0, 0, 0))
    return pl.pallas_call(
        _resid_ln_kernel,
        out_shape=jax.ShapeDtypeStruct((NB, TT, D), F32),
        grid=(NB // nb, TT // tt),
        in_specs=[
            pl.BlockSpec((nb, tt, D), seq),
            pl.BlockSpec((TOP_K, nb, tt, D), lambda b, t: (0, b, t, 0)),
            pl.BlockSpec((nb, 1, D), per),
            vec, vec,
        ],
        out_specs=pl.BlockSpec((nb, tt, D), seq),
        compiler_params=_params("arbitrary", "arbitrary"),
        name="resid_ln",
    )(x, y2, gate, lng, lnb)


def _modproj_kernel(x_ref, sc_ref, sh_ref, w_ref, o_ref):
    nb, tt, d = x_ref.shape
    h = _modulate(x_ref, sc_ref, sh_ref)
    out = jnp.dot(h.astype(BF16), w_ref[...], preferred_element_type=F32)
    o_ref[...] = out.reshape(nb, tt, out.shape[-1]).astype(o_ref.dtype)


def _modproj(x, sc, sh, w, nb, tt, out_dtype):
    NB, TT, D = x.shape
    N = w.shape[1]
    seq = lambda b, t: (b, t, 0)
    per = lambda b, t: (b, 0, 0)
    return pl.pallas_call(
        _modproj_kernel,
        out_shape=jax.ShapeDtypeStruct((NB, TT, N), out_dtype),
        grid=(NB // nb, TT // tt),
        in_specs=[
            pl.BlockSpec((nb, tt, D), seq),
            pl.BlockSpec((nb, 1, D), per),
            pl.BlockSpec((nb, 1, D), per),
            pl.BlockSpec(w.shape, lambda b, t: (0, 0)),
        ],
        out_specs=pl.BlockSpec((nb, tt, N), seq),
        compiler_params=_params("arbitrary", "arbitrary"),
        name="modproj",
    )(x, sc, sh, w)


def _router_kernel(x_ref, sc_ref, sh_ref, w_ref, b_ref, h_ref, lg_ref):
    nb, tt, d = x_ref.shape
    h = _modulate(x_ref, sc_ref, sh_ref)
    h_ref[...] = h.astype(BF16)
    lg_ref[...] = _mm(h, w_ref[...], NN, passes=3) + b_ref[...]


def _router(x, sc, sh, w, b, nb, tt):
    NB, TT, D = x.shape
    steps_t = TT // tt
    seq = lambda b_, t: (b_, t, 0)
    per = lambda b_, t: (b_, 0, 0)
    return pl.pallas_call(
        _router_kernel,
        out_shape=(jax.ShapeDtypeStruct((NB * TT, D), BF16), jax.ShapeDtypeStruct((NB * TT, LANES), F32)),
        grid=(NB // nb, steps_t),
        in_specs=[
            pl.BlockSpec((nb, tt, D), seq),
            pl.BlockSpec((nb, 1, D), per),
            pl.BlockSpec((nb, 1, D), per),
            pl.BlockSpec(w.shape, lambda b_, t: (0, 0)),
            pl.BlockSpec(b.shape, lambda b_, t: (0, 0)),
        ],
        out_specs=(pl.BlockSpec((nb * tt, D), lambda b_, t: (b_ * steps_t + t, 0)),
                   pl.BlockSpec((nb * tt, LANES), lambda b_, t: (b_ * steps_t + t, 0))),
        compiler_params=_params("arbitrary", "arbitrary"),
        name="router",
    )(x, sc, sh, w, b)


def _moe_kernel(be_ref, used_ref, x_ref, w1_ref, w3_ref, w2_ref, gate_ref, o_ref, w1s, w3s, w2s):
    i = pl.program_id(0)
    e = be_ref[i]
    e_prev = be_ref[jnp.maximum(i - 1, 0)]

    @pl.when((i == 0) | (e != e_prev))
    def _():
        w1s[...] = w1_ref[0].astype(BF16)
        w3s[...] = w3_ref[0].astype(BF16)
        w2s[...] = w2_ref[0].astype(BF16)

    @pl.when(i < used_ref[0])
    def _():
        x = x_ref[...]
        h1 = jnp.dot(x, w1s[...], preferred_element_type=F32)
        h3 = jnp.dot(x, w3s[...], preferred_element_type=F32)
        act = (h1 * _sigmoid(h1) * h3).astype(BF16)
        y = jnp.dot(act, w2s[...], preferred_element_type=F32)
        o_ref[...] = y * gate_ref[...]

    @pl.when(i >= used_ref[0])
    def _():
        o_ref[...] = jnp.zeros(o_ref.shape, F32)


def _moe_ffn(blk_expert, n_used, xb, w1, w3, w2, row_gate):
    rows, D = xb.shape
    E, _, DE = w1.shape
    n_blocks = rows // MOE_ROWS
    return pl.pallas_call(
        _moe_kernel,
        out_shape=jax.ShapeDtypeStruct((rows, D), F32),
        grid_spec=pltpu.PrefetchScalarGridSpec(
            num_scalar_prefetch=2,
            grid=(n_blocks,),
            in_specs=[
                pl.BlockSpec((MOE_ROWS, D), lambda i, be, nu: (i, 0)),
                pl.BlockSpec((1, D, DE), lambda i, be, nu: (be[i], 0, 0)),
                pl.BlockSpec((1, D, DE), lambda i, be, nu: (be[i], 0, 0)),
                pl.BlockSpec((1, DE, D), lambda i, be, nu: (be[i], 0, 0)),
                pl.BlockSpec((MOE_ROWS, 1), lambda i, be, nu: (i, 0)),
            ],
            out_specs=pl.BlockSpec((MOE_ROWS, D), lambda i, be, nu: (i, 0)),
            scratch_shapes=[pltpu.VMEM((D, DE), BF16), pltpu.VMEM((D, DE), BF16), pltpu.VMEM((DE, D), BF16)],
        ),
        compiler_params=_params("arbitrary"),
        name="moe_ffn",
    )(blk_expert, n_used, xb, w1, w3, w2, row_gate)


def _attn_kernel(sinks_ref, q_ref, *refs, n_band, first_key_chunk):
    k_refs = refs[:n_band]
    v_refs = refs[n_band:2 * n_band]
    o_ref = refs[2 * n_band]
    tq = q_ref.shape[1]
    kd = jnp.concatenate([r[0] for r in k_refs], axis=0) if n_band > 1 else k_refs[0][0]
    vd = jnp.concatenate([r[0] for r in v_refs], axis=0) if n_band > 1 else v_refs[0][0]
    nk = kd.shape[0]
    qi = lax.broadcasted_iota(jnp.int32, (tq, nk), 0)
    kj = lax.broadcasted_iota(jnp.int32, (tq, nk), 1)
    dist = jnp.abs((nk - tq) + qi - kj).astype(F32)
    if first_key_chunk is not None:
        k_pos = (pl.program_id(1) + first_key_chunk) * tq + kj
        valid = k_pos >= 0
    lane = lax.broadcasted_iota(jnp.int32, (1, LANES), 1)
    n_q_heads = q_ref.shape[2] // HEAD
    q_per_kv = n_q_heads // N_KV_HEADS
    scale = HEAD ** -0.5
    for g in range(N_KV_HEADS):
        kg = kd[:, g * LANES:(g + 1) * LANES]
        vg = vd[:, g * LANES:(g + 1) * LANES]
        zero = jnp.zeros_like(kg)
        k_half = [jnp.where(lane < HEAD, kg, zero), jnp.where(lane >= HEAD, kg, zero)]
        v_half = [jnp.where(lane < HEAD, vg, zero), jnp.where(lane >= HEAD, vg, zero)]
        for pp in range(q_per_kv // PAIR):
            pair = g * (q_per_kv // PAIR) + pp
            qp = q_ref[0, :, pair * LANES:(pair + 1) * LANES].astype(BF16)
            o = jnp.zeros((tq, LANES), F32)
            for hh in range(PAIR):
                h = pair * PAIR + hh
                slope = 2.0 ** (-8.0 * (h + 1) / n_q_heads)
                s = lax.dot_general(qp, k_half[hh], NT, preferred_element_type=F32) * scale - slope * dist
                if first_key_chunk is not None:
                    s = jnp.where(valid, s, NEG_INF)
                sink = sinks_ref[h]
                mx = jnp.maximum(jnp.max(s, axis=-1, keepdims=True), sink)
                pr = jnp.exp(s - mx)
                den = jnp.sum(pr, axis=-1, keepdims=True) + jnp.exp(sink - mx)
                pr = (pr / den).astype(BF16)
                o = o + jnp.dot(pr, v_half[hh], preferred_element_type=F32)
            o_ref[0, :, pair * LANES:(pair + 1) * LANES] = o.astype(o_ref.dtype)


def _attn_prompt(sinks, q, kd, vd):
    B, T, D = q.shape
    nc = T // CHUNK
    w_chunks = WINDOW // CHUNK
    n_band = w_chunks + 1
    KW = kd.shape[-1]
    band = [pl.BlockSpec((1, CHUNK, KW), (lambda b, c, s, j=j: (b, jnp.maximum(c - w_chunks + j, 0), 0)))
            for j in range(n_band)]
    return pl.pallas_call(
        functools.partial(_attn_kernel, n_band=n_band, first_key_chunk=-w_chunks),
        out_shape=jax.ShapeDtypeStruct((B, T, D), BF16),
        grid_spec=pltpu.PrefetchScalarGridSpec(
            num_scalar_prefetch=1,
            grid=(B, nc),
            in_specs=[pl.BlockSpec((1, CHUNK, D), lambda b, c, s: (b, c, 0))] + band + band,
            out_specs=pl.BlockSpec((1, CHUNK, D), lambda b, c, s: (b, c, 0)),
        ),
        compiler_params=_params("arbitrary", "arbitrary"),
        name="attn_prompt",
    )(sinks, q, *([kd] * n_band), *([vd] * n_band))


def _attn_sample(sinks, q, kd, vd):
    B, T, D = q.shape
    NK, KW = kd.shape[1:]
    return pl.pallas_call(
        functools.partial(_attn_kernel, n_band=1, first_key_chunk=None),
        out_shape=jax.ShapeDtypeStruct((B, T, D), BF16),
        grid_spec=pltpu.PrefetchScalarGridSpec(
            num_scalar_prefetch=1,
            grid=(B, 1),
            in_specs=[pl.BlockSpec((1, T, D), lambda b, c, s: (b, 0, 0)),
                      pl.BlockSpec((1, NK, KW), lambda b, c, s: (b, 0, 0)),
                      pl.BlockSpec((1, NK, KW), lambda b, c, s: (b, 0, 0))],
            out_specs=pl.BlockSpec((1, T, D), lambda b, c, s: (b, 0, 0)),
        ),
        compiler_params=_params("arbitrary", "arbitrary"),
        name="attn_sample",
    )(sinks, q, kd, vd)


def _dispatch(logits):
    n = logits.shape[0]
    g_logits = logits[:, :N_GROUPS]
    g_sel = jnp.argmax(g_logits, -1).astype(jnp.int32)
    g_prob = jnp.take_along_axis(jax.nn.softmax(g_logits, -1), g_sel[:, None], -1)
    e_logits = logits[:, N_GROUPS:N_GROUPS + N_EXPERTS].reshape(n, N_GROUPS, EXPERTS_PER_GROUP)
    e_sel = jnp.take_along_axis(e_logits, g_sel[:, None, None], axis=1)[:, 0]
    top_v, top_i = lax.top_k(e_sel, TOP_K)
    gates = jax.nn.softmax(top_v, -1) * g_prob
    expert = g_sel[:, None] * EXPERTS_PER_GROUP + top_i.astype(jnp.int32)

    na = n * TOP_K
    eid = expert.reshape(na)
    onehot = (eid[:, None] == jnp.arange(N_EXPERTS, dtype=jnp.int32)[None, :]).astype(jnp.int32)
    csum = jnp.cumsum(onehot, axis=0)
    rank = jnp.take_along_axis(csum, eid[:, None], axis=1)[:, 0] - 1
    counts = csum[-1]
    padded = (counts + MOE_ROWS - 1) // MOE_ROWS * MOE_ROWS
    pend = jnp.cumsum(padded)
    pstart = pend - padded
    dest = pstart[eid] + rank
    n_blocks = -(-na // MOE_ROWS) + N_EXPERTS
    rows = n_blocks * MOE_ROWS
    tok = jnp.repeat(jnp.arange(n, dtype=jnp.int32), TOP_K)
    row_tok = jnp.full((rows,), n, jnp.int32).at[dest].set(tok)
    row_gate = jnp.zeros((rows,), F32).at[dest].set(gates.reshape(na))
    blk_start = jnp.arange(n_blocks, dtype=jnp.int32) * MOE_ROWS
    blk_expert = jnp.minimum(jnp.searchsorted(pend, blk_start, side='right'), N_EXPERTS - 1).astype(jnp.int32)
    n_used = (pend[-1] // MOE_ROWS).astype(jnp.int32).reshape(1)
    return row_tok, row_gate, blk_expert, n_used, dest.reshape(n, TOP_K)


def _moe_layer(xs, mods, w_router, b_router, w1, w3, w2, cfgs):
    hs, lgs = [], []
    for x, (sh, sc), (nb, tt) in zip(xs, mods, cfgs):
        h, lg = _router(x, sc, sh, w_router, b_router, nb, tt)
        hs.append(h)
        lgs.append(lg)
    h_all = jnp.concatenate(hs, axis=0)
    lg_all = jnp.concatenate(lgs, axis=0)
    row_tok, row_gate, blk_expert, n_used, dest = _dispatch(lg_all)
    xb = jnp.take(h_all, row_tok, axis=0, mode='fill', fill_value=0)
    yb = _moe_ffn(blk_expert, n_used, xb, w1, w3, w2, row_gate[:, None])
    outs = []
    row = 0
    for x in xs:
        NB, TT, D = x.shape
        d = dest[row:row + NB * TT]
        y2 = jnp.stack([jnp.take(yb, d[:, j], axis=0) for j in range(TOP_K)]).reshape(TOP_K, NB, TT, D)
        outs.append(y2)
        row += NB * TT
    return outs


def _dup_heads(t):
    B, T, _ = t.shape
    t4 = t.reshape(B, T, N_KV_HEADS, 1, HEAD)
    return jnp.broadcast_to(t4, (B, T, N_KV_HEADS, PAIR, HEAD)).reshape(B, T, N_KV_HEADS * LANES).astype(BF16)


def _pad_cols(w, n):
    return jnp.pad(w, ((0, 0), (0, n - w.shape[1])))


def _pad_rows(w, n):
    return jnp.pad(w, ((0, n - w.shape[0]), (0, 0)))


WKV_PASSES = 6


def kernel(x_prompt, x_sample, c_prompt, c_sample, state_wkv, state_shift, cache_k, cache_v, mod_w, mod_b, ln_g, ln_b, rw_mu, rw_wr, rw_wk, rw_wv, rw_w0, rw_w1, rw_w2, rw_a0, rw_a1, rw_a2, rw_g1, rw_g2, rw_kk, rw_ka, rw_rk, rw_lnx_g, rw_lnx_b, rw_wo, kv_mod_w, kv_mod_b, w_kv, at_wq, at_sinks, at_wo, moe_wg, moe_bg, moe_wr, moe_br, moe_w1, moe_w3, moe_w2):
    B, T, D = x_prompt.shape
    BS, TS, _ = x_sample.shape
    H = D // HEAD
    KVW = N_KV_HEADS * HEAD
    cfgs = [(1, min(PROMPT_ROWS, T)), (BS, TS)]
    chunk_len = [CHUNK, TS]

    c_all = jnp.concatenate([c_prompt, c_sample], axis=0)
    m_all = _mods(c_all, mod_w.reshape(DEPTH * 2, D, 3 * D), mod_b.reshape(DEPTH * 2, 1, 3 * D))
    kvm = _mods(c_all, kv_mod_w[None], kv_mod_b[None, None])[0]
    rows = [slice(0, B), slice(B, B + BS)]

    def mod(layer, sub, part, path):
        return m_all[layer * 2 + sub, rows[path], part * D:(part + 1) * D][:, None, :]

    xs = [x_prompt, x_sample]
    shift0 = [jnp.zeros((B, 1, D), F32), state_shift[0][:, None, :]]
    wkv0 = [jnp.zeros((B, H, HEAD, HEAD), F32), state_wkv[0]]

    mu = rw_mu[0]
    mu_rkv = jnp.stack([mu[0], mu[2], mu[3]])[:, None, :]
    mu_lora = jnp.stack([mu[1], mu[4], mu[5]])
    w_rkv = jnp.stack([rw_wr[0], rw_wk[0], rw_wv[0]]).astype(BF16)
    lw1 = _pad_cols(rw_w1[0], LANES).astype(BF16)
    lw2 = _pad_rows(rw_w2[0], LANES).astype(BF16)
    la1 = _pad_cols(rw_a1[0], LANES).astype(BF16)
    la2 = _pad_rows(rw_a2[0], LANES).astype(BF16)
    lg1 = rw_g1[0].astype(BF16)
    lg2 = rw_g2[0].astype(BF16)
    wo = rw_wo[0].astype(BF16)
    vec = lambda v: v.reshape(1, D)
    new_wkv, new_shift, x1 = [], [], []
    for p in range(2):
        nb, tt = cfgs[p]
        sh, sc, gt = mod(0, 0, 0, p), mod(0, 0, 1, p), mod(0, 0, 2, p)
        rkv = _rkv(xs[p], sc, sh, shift0[p], mu_rkv, w_rkv, nb, tt)
        lw, a, g, last = _lora(xs[p], sc, sh, shift0[p], mu_lora, lw1, lw2, vec(rw_w0[0]), la1, la2, vec(rw_a0[0]),
                               lg1, lg2, nb, tt)
        z, s_fin = _wkv(rkv, lw, a, g, vec(rw_kk[0]), vec(rw_ka[0]), vec(rw_rk[0]), vec(rw_lnx_g[0]),
                        vec(rw_lnx_b[0]), wkv0[p], chunk_len[p], WKV_PASSES)
        x1.append(_outproj_ln(z, xs[p], gt, ln_g[0, 0].reshape(1, 1, D), ln_b[0, 0].reshape(1, 1, D), wo, nb, tt))
        new_wkv.append(s_fin[None])
        new_shift.append(last.reshape(1, -1, D))

    def moe(layer, xin):
        w_router = _pad_cols(jnp.concatenate([moe_wg[layer], moe_wr[layer]], axis=1), LANES)
        b_router = _pad_cols(jnp.concatenate([moe_bg[layer], moe_br[layer]])[None, :], LANES)
        mods = [(mod(layer, 1, 0, p), mod(layer, 1, 1, p)) for p in range(2)]
        ys = _moe_layer(xin, mods, w_router, b_router, moe_w1[layer], moe_w3[layer], moe_w2[layer], cfgs)
        out = []
        for p in range(2):
            nb, tt = cfgs[p]
            out.append(_resid_ln(xin[p], ys[p], mod(layer, 1, 2, p), ln_g[layer, 1].reshape(1, 1, D),
                                 ln_b[layer, 1].reshape(1, 1, D), nb, tt))
        return out

    x2 = moe(0, x1)

    w_kv_b = w_kv.astype(BF16)
    kv = []
    for p in range(2):
        nb, tt = cfgs[p]
        kv_sh = kvm[rows[p], :D][:, None, :]
        kv_sc = kvm[rows[p], D:][:, None, :]
        kv.append(_modproj(x2[p], kv_sc, kv_sh, w_kv_b, nb, tt, F32))
    k_sh = [t[..., :KVW] for t in kv]
    v_sh = [t[..., KVW:] for t in kv]

    wq = at_wq[0].astype(BF16)
    wo1 = at_wo[0].astype(BF16)
    sinks = at_sinks[0].astype(F32)
    x3 = []
    for p in range(2):
        nb, tt = cfgs[p]
        sh, sc, gt = mod(1, 0, 0, p), mod(1, 0, 1, p), mod(1, 0, 2, p)
        q = _modproj(x2[p], sc, sh, wq, nb, tt, F32)
        if p == 0:
            o = _attn_prompt(sinks, q, _dup_heads(k_sh[p]), _dup_heads(v_sh[p]))
        else:
            k_all = jnp.concatenate([cache_k.reshape(BS, WINDOW, KVW), k_sh[p]], axis=1)
            v_all = jnp.concatenate([cache_v.reshape(BS, WINDOW, KVW), v_sh[p]], axis=1)
            o = _attn_sample(sinks, q, _dup_heads(k_all), _dup_heads(v_all))
        x3.append(_outproj_ln(o, x2[p], gt, ln_g[1, 0].reshape(1, 1, D), ln_b[1, 0].reshape(1, 1, D), wo1, nb, tt))

    x4 = moe(1, x3)

    k_p = k_sh[0][:, -WINDOW:].reshape(B, WINDOW, N_KV_HEADS, HEAD)
    v_p = v_sh[0][:, -WINDOW:].reshape(B, WINDOW, N_KV_HEADS, HEAD)
    k_s = k_sh[1].reshape(BS, TS, N_KV_HEADS, HEAD)
    v_s = v_sh[1].reshape(BS, TS, N_KV_HEADS, HEAD)
    return (x4[0], x4[1], new_wkv[0], new_shift[0], k_p, v_p, new_wkv[1], new_shift[1], k_s, v_s)
```

```python
import functools

import jax
import jax.numpy as jnp
from jax import lax
from jax.experimental import pallas as pl
from jax.experimental.pallas import tpu as pltpu

F32 = jnp.float32
BF16 = jnp.bfloat16

HEAD = 64
N_KV_HEADS = 4
WINDOW = 128
CHUNK = 64
N_GROUPS = 4
EXPERTS_PER_GROUP = 8
N_EXPERTS = N_GROUPS * EXPERTS_PER_GROUP
TOP_K = 2
DEPTH = 2
ALPHA = (2.0 * DEPTH) ** 0.25
LN_EPS = 1e-5
GN_EPS = 64e-5
NEG_INF = -1e30

LANES = 128
PAIR = LANES // HEAD

PROMPT_ROWS = 512
MOE_ROWS = 256
MODS_TN = 1024
WKV_PASSES = 1
WKV_UNROLL = 4
VMEM_LIMIT = 56 * 1024 * 1024

NN = (((1,), (0,)), ((), ()))
NT = (((1,), (1,)), ((), ()))
TN = (((0,), (0,)), ((), ()))


def _mm(a, b, dims=NN, passes=1):
    if passes == 6:
        return lax.dot_general(a, b, dims, precision=lax.Precision.HIGHEST, preferred_element_type=F32)
    ah = a.astype(BF16)
    bh = b.astype(BF16)
    out = lax.dot_general(ah, bh, dims, preferred_element_type=F32)
    if passes == 3:
        al = (a - ah.astype(F32)).astype(BF16)
        bl = (b - bh.astype(F32)).astype(BF16)
        out = out + lax.dot_general(ah, bl, dims, preferred_element_type=F32)
        out = out + lax.dot_general(al, bh, dims, preferred_element_type=F32)
    return out


def _bf16_terms(x, terms):
    out = []
    for _ in range(terms):
        t = x.astype(BF16)
        out.append(t)
        x = x - t.astype(F32)
    return out


def _mm_mask(a, b, dims=NN, split="lhs", terms=2):
    if split == "lhs":
        bb = b.astype(BF16)
        parts = [lax.dot_general(t, bb, dims, preferred_element_type=F32) for t in _bf16_terms(a, terms)]
    else:
        aa = a.astype(BF16)
        parts = [lax.dot_general(aa, t, dims, preferred_element_type=F32) for t in _bf16_terms(b, terms)]
    return functools.reduce(lambda p, q: p + q, parts)


def _sigmoid(x):
    return 1.0 / (1.0 + jnp.exp(-x))


def _layer_norm(v, g, b):
    mu = jnp.mean(v, axis=-1, keepdims=True)
    d = v - mu
    var = jnp.mean(d * d, axis=-1, keepdims=True)
    return d * lax.rsqrt(var + LN_EPS) * g + b


def _params(*sem):
    return pltpu.CompilerParams(dimension_semantics=sem, vmem_limit_bytes=VMEM_LIMIT)


def _mods_kernel(c_ref, w_ref, b_ref, o_ref):
    c = c_ref[...]
    a = (c * _sigmoid(c)).astype(BF16)
    o_ref[0] = jnp.dot(a, w_ref[0].astype(BF16), preferred_element_type=F32) + b_ref[0]


def _mods(c, w, b):
    g, d, n = w.shape
    m = c.shape[0]
    tn = min(MODS_TN, n)
    assert n % tn == 0
    return pl.pallas_call(
        _mods_kernel,
        out_shape=jax.ShapeDtypeStruct((g, m, n), F32),
        grid=(g, n // tn),
        in_specs=[
            pl.BlockSpec((m, d), lambda i, j: (0, 0)),
            pl.BlockSpec((1, d, tn), lambda i, j: (i, 0, j)),
            pl.BlockSpec((1, 1, tn), lambda i, j: (i, 0, j)),
        ],
        out_specs=pl.BlockSpec((1, m, tn), lambda i, j: (i, 0, j)),
        compiler_params=_params("arbitrary", "arbitrary"),
        name="mods",
    )(c, w, b)


def _modulate(x_ref, sc_ref, sh_ref):
    nb, tt, d = x_ref.shape
    h = x_ref[...] * (1.0 + sc_ref[...]) + sh_ref[...]
    return h.reshape(nb * tt, d)


def _token_shift(h, prev_scr, prev0_ref, tt, t_axis):
    rows, d = h.shape
    nb = rows // tt

    @pl.when(pl.program_id(t_axis) == 0)
    def _():
        prev_scr[...] = prev0_ref[...]

    prev = jnp.broadcast_to(prev_scr[...], (nb, tt, d)).reshape(rows, d)
    rolled = pltpu.roll(h, 1, 0)
    row = lax.broadcasted_iota(jnp.int32, (rows, 1), 0)
    first = (row & (tt - 1)) == 0
    hp = jnp.where(first, prev, rolled)
    prev_scr[...] = h.reshape(nb, tt, d)[:, tt - 1:tt, :]
    return hp


def _rkv_kernel(x_ref, sc_ref, sh_ref, prev0_ref, mu_ref, w_ref, o_ref, prev_scr):
    nb, tt, d = x_ref.shape
    h = _modulate(x_ref, sc_ref, sh_ref)
    hp = _token_shift(h, prev_scr, prev0_ref, tt, 2)
    xin = h + (hp - h) * mu_ref[0]
    out = jnp.dot(xin.astype(BF16), w_ref[0], preferred_element_type=F32)
    o_ref[0] = out.reshape(nb, tt, out.shape[-1])


def _rkv(x, sc, sh, prev0, mu3, w3, nb, tt):
    NB, TT, D = x.shape
    seq = lambda j, b, t: (b, t, 0)
    per = lambda j, b, t: (b, 0, 0)
    return pl.pallas_call(
        _rkv_kernel,
        out_shape=jax.ShapeDtypeStruct((3, NB, TT, D), F32),
        grid=(3, NB // nb, TT // tt),
        in_specs=[
            pl.BlockSpec((nb, tt, D), seq),
            pl.BlockSpec((nb, 1, D), per),
            pl.BlockSpec((nb, 1, D), per),
            pl.BlockSpec((nb, 1, D), per),
            pl.BlockSpec((1, 1, D), lambda j, b, t: (j, 0, 0)),
            pl.BlockSpec((1, D, D), lambda j, b, t: (j, 0, 0)),
        ],
        out_specs=pl.BlockSpec((1, nb, tt, D), lambda j, b, t: (j, b, t, 0)),
        scratch_shapes=[pltpu.VMEM((nb, 1, D), F32)],
        compiler_params=_params("arbitrary", "arbitrary", "arbitrary"),
        name="tmix_rkv",
    )(x, sc, sh, prev0, mu3, w3)


def _lora_kernel(x_ref, sc_ref, sh_ref, prev0_ref, mu_ref, w1_ref, w2_ref, w0_ref, a1_ref, a2_ref, a0_ref,
                 g1_ref, g2_ref, lw_ref, a_ref, g_ref, last_ref, prev_scr):
    nb, tt, d = x_ref.shape
    h = _modulate(x_ref, sc_ref, sh_ref)
    hp = _token_shift(h, prev_scr, prev0_ref, tt, 1)
    dx = hp - h
    last_ref[...] = h.reshape(nb, tt, d)[:, tt - 1:tt, :]

    xw = (h + dx * mu_ref[0:1, :]).astype(BF16)
    zw = jnp.tanh(jnp.dot(xw, w1_ref[...], preferred_element_type=F32))
    zw = jnp.dot(zw.astype(BF16), w2_ref[...], preferred_element_type=F32) + w0_ref[...]
    sp = jnp.maximum(-zw, 0.0) + jnp.log1p(jnp.exp(-jnp.abs(zw)))
    lw_ref[...] = (-jnp.exp(-sp - 0.5)).reshape(nb, tt, d)

    xa = (h + dx * mu_ref[1:2, :]).astype(BF16)
    za = jnp.dot(xa, a1_ref[...], preferred_element_type=F32)
    za = jnp.dot(za.astype(BF16), a2_ref[...], preferred_element_type=F32) + a0_ref[...]
    a_ref[...] = _sigmoid(za).reshape(nb, tt, d)

    xg = (h + dx * mu_ref[2:3, :]).astype(BF16)
    zg = _sigmoid(jnp.dot(xg, g1_ref[...], preferred_element_type=F32))
    g_ref[...] = jnp.dot(zg.astype(BF16), g2_ref[...], preferred_element_type=F32).reshape(nb, tt, d)


def _lora(x, sc, sh, prev0, mu3, w1, w2, w0, a1, a2, a0, g1, g2, nb, tt):
    NB, TT, D = x.shape
    seq = lambda b, t: (b, t, 0)
    per = lambda b, t: (b, 0, 0)
    full = lambda a: pl.BlockSpec(a.shape, lambda b, t: (0,) * a.ndim)
    big = jax.ShapeDtypeStruct((NB, TT, D), F32)
    return pl.pallas_call(
        _lora_kernel,
        out_shape=(big, big, big, jax.ShapeDtypeStruct((NB, 1, D), F32)),
        grid=(NB // nb, TT // tt),
        in_specs=[
            pl.BlockSpec((nb, tt, D), seq),
            pl.BlockSpec((nb, 1, D), per),
            pl.BlockSpec((nb, 1, D), per),
            pl.BlockSpec((nb, 1, D), per),
            full(mu3), full(w1), full(w2), full(w0), full(a1), full(a2), full(a0), full(g1), full(g2),
        ],
        out_specs=(pl.BlockSpec((nb, tt, D), seq),) * 3 + (pl.BlockSpec((nb, 1, D), per),),
        scratch_shapes=[pltpu.VMEM((nb, 1, D), F32)],
        compiler_params=_params("arbitrary", "arbitrary"),
        name="tmix_lora",
    )(x, sc, sh, prev0, mu3, w1, w2, w0, a1, a2, a0, g1, g2)


def _wkv_kernel(r_ref, k_ref, v_ref, lw_ref, a_ref, g_ref, kkc_ref, kac_ref, rk_ref, lng_ref, lnb_ref, s0_ref,
                z_ref, sfin_ref, s_scr, *, passes, unroll):
    L = r_ref.shape[2]
    n_heads = s_scr.shape[0]
    c = pl.program_id(1)

    @pl.when(c == 0)
    def _():
        s_scr[...] = jnp.zeros(s_scr.shape, F32)
        for h in range(n_heads):
            o = (h % PAIR) * HEAD
            s_scr[h, o:o + HEAD, o:o + HEAD] = s0_ref[0, h]

    lane = lax.broadcasted_iota(jnp.int32, (1, LANES), 1)
    ri = lax.broadcasted_iota(jnp.int32, (L, L), 0)
    ci = lax.broadcasted_iota(jnp.int32, (L, L), 1)
    strict = ri > ci
    incl = ri >= ci
    tril = jnp.where(incl, 1.0, 0.0).astype(F32)
    bi = lax.broadcasted_iota(jnp.int32, (LANES, LANES), 0) // HEAD
    bj = lax.broadcasted_iota(jnp.int32, (LANES, LANES), 1) // HEAD
    head_ones = jnp.where(bi == bj, 1.0, 0.0).astype(F32)

    def group_body(gi, carry):
        pairs = [gi * unroll + u for u in range(unroll)]
        sls = [pl.ds(pl.multiple_of(p * LANES, LANES), LANES) for p in pairs]
        heads = [(i, hh) for i in range(unroll) for hh in range(PAIR)]
        masks = [lane < HEAD, lane >= HEAD]
        s0 = [s_scr[PAIR * pairs[i] + hh] for i, hh in heads]

        rp = [r_ref[0, 0, :, sl] for sl in sls]
        kp = [k_ref[0, 0, :, sl] for sl in sls]
        vp = [v_ref[0, 0, :, sl] for sl in sls]
        lwp = [lw_ref[0, :, sl] for sl in sls]
        ap = [a_ref[0, :, sl] for sl in sls]
        cum = [_mm_mask(tril, t, split="rhs", terms=3) for t in lwp]
        kk0 = [k * kkc_ref[:, sl] for k, sl in zip(kp, sls)]
        n2 = [_mm_mask(t * t, head_ones) for t in kk0]
        kkp = [t * lax.rsqrt(jnp.maximum(n, 1e-24)) for t, n in zip(kk0, n2)]
        khp = [k * (1.0 + (a - 1.0) * kac_ref[:, sl]) for k, a, sl in zip(kp, ap, sls)]
        c_incl = [jnp.exp(t) for t in cum]
        c_inv = [jnp.exp(-t) for t in cum]
        kt = [kk * jnp.exp(t - lw) for kk, t, lw in zip(kkp, cum, lwp)]
        rt = [r * ci_ for r, ci_ in zip(rp, c_incl)]
        kti = [kh * cv for kh, cv in zip(khp, c_inv)]
        bti = [kk * a * cv for kk, a, cv in zip(kkp, ap, c_inv)]
        lhs = [jnp.concatenate([a_, b_], axis=0) for a_, b_ in zip(kt, rt)]

        lhs_m = [jnp.where(masks[hh], lhs[i], 0.0) for i, hh in heads]
        gk = [_mm(lhs_m[j], kti[i], NT, passes) for j, (i, hh) in enumerate(heads)]
        gb = [_mm(lhs_m[j], bti[i], NT, passes) for j, (i, hh) in enumerate(heads)]
        p0 = [_mm(lhs[i], s0[j], NT, passes) for j, (i, hh) in enumerate(heads)]
        a_kk = [jnp.where(strict, t[:L], 0.0) for t in gk]
        a_rk = [jnp.where(incl, t[L:], 0.0) for t in gk]
        a_rb = [jnp.where(incl, t[L:], 0.0) for t in gb]
        pw = [jnp.where(strict, -t[:L], 0.0) for t in gb]
        x = [p0[j][:L] + _mm(a_kk[j], vp[i], NN, passes) for j, (i, hh) in enumerate(heads)]
        u = [x_ + _mm(p_, x_, NN, passes) for p_, x_ in zip(pw, x)]
        n = 1
        while n * 2 < L:
            pw = [_mm(p_, p_, NN, passes) for p_ in pw]
            u = [u_ + _mm(p_, u_, NN, passes) for p_, u_ in zip(pw, u)]
            n *= 2
        y = [p0[j][L:] + _mm(a_rk[j], vp[i], NN, passes) - _mm(a_rb[j], u[j], NN, passes)
             for j, (i, hh) in enumerate(heads)]
        vu = [jnp.where(masks[hh], jnp.concatenate([vp[i], u[j]], axis=0), 0.0) for j, (i, hh) in enumerate(heads)]
        kb = [jnp.where(masks[hh], jnp.concatenate([kti[i], -bti[i]], axis=0), 0.0) for i, hh in heads]
        ds = [_mm(vu_, kb_, TN, passes) for vu_, kb_ in zip(vu, kb)]
        s_new = [(s0[j] + ds[j]) * c_incl[i][L - 1:L, :] for j, (i, hh) in enumerate(heads)]

        yp = [jnp.where(masks[0], y[PAIR * i], y[PAIR * i + 1]) for i in range(unroll)]
        mean = [_mm_mask(t, head_ones) * (1.0 / HEAD) for t in yp]
        dlt = [t - m_ for t, m_ in zip(yp, mean)]
        var = [_mm_mask(t * t, head_ones) * (1.0 / HEAD) for t in dlt]
        bonus = [_mm_mask(r * kh * rk_ref[:, sl], head_ones) * v
                 for r, kh, v, sl in zip(rp, khp, vp, sls)]
        for i, sl in enumerate(sls):
            yn = dlt[i] * lax.rsqrt(var[i] + GN_EPS) * lng_ref[:, sl] + lnb_ref[:, sl]
            z_ref[0, :, sl] = ((yn + bonus[i]) * g_ref[0, :, sl]).astype(z_ref.dtype)
        for j, (i, hh) in enumerate(heads):
            s_scr[PAIR * pairs[i] + hh] = s_new[j]
        return carry

    assert (n_heads // PAIR) % unroll == 0
    lax.fori_loop(0, n_heads // PAIR // unroll, group_body, 0)

    @pl.when(c == pl.num_programs(1) - 1)
    def _():
        for h in range(n_heads):
            o = (h % PAIR) * HEAD
            sfin_ref[0, h] = s_scr[h, o:o + HEAD, o:o + HEAD]


def _wkv(rkv, lw, a, g, kkc, kac, rk, lng, lnb, s0, L, passes=WKV_PASSES, unroll=WKV_UNROLL):
    _, NB, TT, D = rkv.shape
    H = D // HEAD
    seq = lambda b, t: (b, t, 0)
    vec = pl.BlockSpec((1, D), lambda b, t: (0, 0))
    st = pl.BlockSpec((1, H, HEAD, HEAD), lambda b, t: (b, 0, 0, 0))
    return pl.pallas_call(
        functools.partial(_wkv_kernel, passes=passes, unroll=unroll),
        out_shape=(jax.ShapeDtypeStruct((NB, TT, D), BF16), jax.ShapeDtypeStruct((NB, H, HEAD, HEAD), F32)),
        grid=(NB, TT // L),
        in_specs=[
            pl.BlockSpec((1, 1, L, D), lambda b, t: (0, b, t, 0)),
            pl.BlockSpec((1, 1, L, D), lambda b, t: (1, b, t, 0)),
            pl.BlockSpec((1, 1, L, D), lambda b, t: (2, b, t, 0)),
            pl.BlockSpec((1, L, D), seq),
            pl.BlockSpec((1, L, D), seq),
            pl.BlockSpec((1, L, D), seq),
            vec, vec, vec, vec, vec, st,
        ],
        out_specs=(pl.BlockSpec((1, L, D), seq), st),
        scratch_shapes=[pltpu.VMEM((H, LANES, LANES), F32)],
        compiler_params=_params("arbitrary", "arbitrary"),
        name="wkv",
    )(rkv, rkv, rkv, lw, a, g, kkc, kac, rk, lng, lnb, s0)


def _outproj_ln_kernel(a_ref, x_ref, gate_ref, lng_ref, lnb_ref, w_ref, o_ref):
    nb, tt, d = x_ref.shape
    a = a_ref[...].reshape(nb * tt, a_ref.shape[-1])
    y = jnp.dot(a, w_ref[...], preferred_element_type=F32).reshape(nb, tt, d)
    v = ALPHA * x_ref[...] + (1.0 + gate_ref[...]) * y
    o_ref[...] = _layer_norm(v, lng_ref[...], lnb_ref[...])


def _outproj_ln(a, x, gate, lng, lnb, w, nb, tt):
    NB, TT, D = x.shape
    seq = lambda b, t: (b, t, 0)
    per = lambda b, t: (b, 0, 0)
    vec = pl.BlockSpec((1, 1, D), lambda b, t: (0, 0, 0))
    return pl.pallas_call(
        _outproj_ln_kernel,
        out_shape=jax.ShapeDtypeStruct((NB, TT, D), F32),
        grid=(NB // nb, TT // tt),
        in_specs=[
            pl.BlockSpec((nb, tt, a.shape[-1]), seq),
            pl.BlockSpec((nb, tt, D), seq),
            pl.BlockSpec((nb, 1, D), per),
            vec, vec,
            pl.BlockSpec(w.shape, lambda b, t: (0, 0)),
        ],
        out_specs=pl.BlockSpec((nb, tt, D), seq),
        compiler_params=_params("arbitrary", "arbitrary"),
        name="outproj_ln",
    )(a, x, gate, lng, lnb, w)


def _resid_ln_kernel(x_ref, y_ref, gate_ref, lng_ref, lnb_ref, o_ref):
    y = y_ref[0] + y_ref[1]
    v = ALPHA * x_ref[...] + (1.0 + gate_ref[...]) * y
    o_ref[...] = _layer_norm(v, lng_ref[...], lnb_ref[...])


def _resid_ln(x, y2, gate, lng, lnb, nb, tt):
    NB, TT, D = x.shape
    seq = lambda b, t: (b, t, 0)
    per = lambda b, t: (b, 0, 0)
    vec = pl.BlockSpec((1, 1, D), lambda b, t: (0, 0, 0))
    return pl.pallas_call(
        _resid_ln_kernel,
        out_shape=jax.ShapeDtypeStruct((NB, TT, D), F32),
        grid=(NB // nb, TT // tt),
        in_specs=[
            pl.BlockSpec((nb, tt, D), seq),
            pl.BlockSpec((TOP_K, nb, tt, D), lambda b, t: (0, b, t, 0)),
            pl.BlockSpec((nb, 1, D), per),
            vec, vec,
        ],
        out_specs=pl.BlockSpec((nb, tt, D), seq),
        compiler_params=_params("arbitrary", "arbitrary"),
        name="resid_ln",
    )(x, y2, gate, lng, lnb)


def _modproj_kernel(x_ref, sc_ref, sh_ref, w_ref, o_ref):
    nb, tt, d = x_ref.shape
    h = _modulate(x_ref, sc_ref, sh_ref)
    out = jnp.dot(h.astype(BF16), w_ref[...], preferred_element_type=F32)
    o_ref[...] = out.reshape(nb, tt, out.shape[-1]).astype(o_ref.dtype)


def _modproj(x, sc, sh, w, nb, tt, out_dtype):
    NB, TT, D = x.shape
    N = w.shape[1]
    seq = lambda b, t: (b, t, 0)
    per = lambda b, t: (b, 0, 0)
    return pl.pallas_call(
        _modproj_kernel,
        out_shape=jax.ShapeDtypeStruct((NB, TT, N), out_dtype),
        grid=(NB // nb, TT // tt),
        in_specs=[
            pl.BlockSpec((nb, tt, D), seq),
            pl.BlockSpec((nb, 1, D), per),
            pl.BlockSpec((nb, 1, D), per),
            pl.BlockSpec(w.shape, lambda b, t: (0, 0)),
        ],
        out_specs=pl.BlockSpec((nb, tt, N), seq),
        compiler_params=_params("arbitrary", "arbitrary"),
        name="modproj",
    )(x, sc, sh, w)


def _router_kernel(x_ref, sc_ref, sh_ref, w_ref, b_ref, h_ref, lg_ref):
    nb, tt, d = x_ref.shape
    h = _modulate(x_ref, sc_ref, sh_ref)
    h_ref[...] = h.astype(BF16)
    lg_ref[...] = _mm(h, w_ref[...], NN, passes=3) + b_ref[...]


def _router(x, sc, sh, w, b, nb, tt):
    NB, TT, D = x.shape
    steps_t = TT // tt
    seq = lambda b_, t: (b_, t, 0)
    per = lambda b_, t: (b_, 0, 0)
    return pl.pallas_call(
        _router_kernel,
        out_shape=(jax.ShapeDtypeStruct((NB * TT, D), BF16), jax.ShapeDtypeStruct((NB * TT, LANES), F32)),
        grid=(NB // nb, steps_t),
        in_specs=[
            pl.BlockSpec((nb, tt, D), seq),
            pl.BlockSpec((nb, 1, D), per),
            pl.BlockSpec((nb, 1, D), per),
            pl.BlockSpec(w.shape, lambda b_, t: (0, 0)),
            pl.BlockSpec(b.shape, lambda b_, t: (0, 0)),
        ],
        out_specs=(pl.BlockSpec((nb * tt, D), lambda b_, t: (b_ * steps_t + t, 0)),
                   pl.BlockSpec((nb * tt, LANES), lambda b_, t: (b_ * steps_t + t, 0))),
        compiler_params=_params("arbitrary", "arbitrary"),
        name="router",
    )(x, sc, sh, w, b)


def _moe_kernel(be_ref, used_ref, x_ref, w1_ref, w3_ref, w2_ref, gate_ref, o_ref, w1s, w3s, w2s):
    i = pl.program_id(0)
    e = be_ref[i]
    e_prev = be_ref[jnp.maximum(i - 1, 0)]

    @pl.when((i == 0) | (e != e_prev))
    def _():
        w1s[...] = w1_ref[0].astype(BF16)
        w3s[...] = w3_ref[0].astype(BF16)
        w2s[...] = w2_ref[0].astype(BF16)

    @pl.when(i < used_ref[0])
    def _():
        x = x_ref[...]
        h1 = jnp.dot(x, w1s[...], preferred_element_type=F32)
        h3 = jnp.dot(x, w3s[...], preferred_element_type=F32)
        act = (h1 * _sigmoid(h1) * h3).astype(BF16)
        y = jnp.dot(act, w2s[...], preferred_element_type=F32)
        o_ref[...] = y * gate_ref[...]

    @pl.when(i >= used_ref[0])
    def _():
        o_ref[...] = jnp.zeros(o_ref.shape, F32)


def _moe_ffn(blk_expert, n_used, xb, w1, w3, w2, row_gate):
    rows, D = xb.shape
    E, _, DE = w1.shape
    n_blocks = rows // MOE_ROWS
    return pl.pallas_call(
        _moe_kernel,
        out_shape=jax.ShapeDtypeStruct((rows, D), F32),
        grid_spec=pltpu.PrefetchScalarGridSpec(
            num_scalar_prefetch=2,
            grid=(n_blocks,),
            in_specs=[
                pl.BlockSpec((MOE_ROWS, D), lambda i, be, nu: (i, 0)),
                pl.BlockSpec((1, D, DE), lambda i, be, nu: (be[i], 0, 0)),
                pl.BlockSpec((1, D, DE), lambda i, be, nu: (be[i], 0, 0)),
                pl.BlockSpec((1, DE, D), lambda i, be, nu: (be[i], 0, 0)),
                pl.BlockSpec((MOE_ROWS, 1), lambda i, be, nu: (i, 0)),
            ],
            out_specs=pl.BlockSpec((MOE_ROWS, D), lambda i, be, nu: (i, 0)),
            scratch_shapes=[pltpu.VMEM((D, DE), BF16), pltpu.VMEM((D, DE), BF16), pltpu.VMEM((DE, D), BF16)],
        ),
        compiler_params=_params("arbitrary"),
        name="moe_ffn",
    )(blk_expert, n_used, xb, w1, w3, w2, row_gate)


def _attn_kernel(sinks_ref, q_ref, *refs, n_band, first_key_chunk):
    k_refs = refs[:n_band]
    v_refs = refs[n_band:2 * n_band]
    o_ref = refs[2 * n_band]
    tq = q_ref.shape[1]
    kd = jnp.concatenate([r[0] for r in k_refs], axis=0) if n_band > 1 else k_refs[0][0]
    vd = jnp.concatenate([r[0] for r in v_refs], axis=0) if n_band > 1 else v_refs[0][0]
    nk = kd.shape[0]
    qi = lax.broadcasted_iota(jnp.int32, (tq, nk), 0)
    kj = lax.broadcasted_iota(jnp.int32, (tq, nk), 1)
    dist = jnp.abs((nk - tq) + qi - kj).astype(F32)
    if first_key_chunk is not None:
        k_pos = (pl.program_id(1) + first_key_chunk) * tq + kj
        valid = k_pos >= 0
    lane = lax.broadcasted_iota(jnp.int32, (1, LANES), 1)
    n_q_heads = q_ref.shape[2] // HEAD
    q_per_kv = n_q_heads // N_KV_HEADS
    scale = HEAD ** -0.5
    for g in range(N_KV_HEADS):
        kg = kd[:, g * LANES:(g + 1) * LANES]
        vg = vd[:, g * LANES:(g + 1) * LANES]
        zero = jnp.zeros_like(kg)
        k_half = [jnp.where(lane < HEAD, kg, zero), jnp.where(lane >= HEAD, kg, zero)]
        v_half = [jnp.where(lane < HEAD, vg, zero), jnp.where(lane >= HEAD, vg, zero)]
        for pp in range(q_per_kv // PAIR):
            pair = g * (q_per_kv // PAIR) + pp
            qp = q_ref[0, :, pair * LANES:(pair + 1) * LANES].astype(BF16)
            o = jnp.zeros((tq, LANES), F32)
            for hh in range(PAIR):
                h = pair * PAIR + hh
                slope = 2.0 ** (-8.0 * (h + 1) / n_q_heads)
                s = lax.dot_general(qp, k_half[hh], NT, preferred_element_type=F32) * scale - slope * dist
                if first_key_chunk is not None:
                    s = jnp.where(valid, s, NEG_INF)
                sink = sinks_ref[h]
                mx = jnp.maximum(jnp.max(s, axis=-1, keepdims=True), sink)
                pr = jnp.exp(s - mx)
                den = jnp.sum(pr, axis=-1, keepdims=True) + jnp.exp(sink - mx)
                pr = (pr / den).astype(BF16)
                o = o + jnp.dot(pr, v_half[hh], preferred_element_type=F32)
            o_ref[0, :, pair * LANES:(pair + 1) * LANES] = o.astype(o_ref.dtype)


def _attn_prompt(sinks, q, kd, vd):
    B, T, D = q.shape
    nc = T // CHUNK
    w_chunks = WINDOW // CHUNK
    n_band = w_chunks + 1
    KW = kd.shape[-1]
    band = [pl.BlockSpec((1, CHUNK, KW), (lambda b, c, s, j=j: (b, jnp.maximum(c - w_chunks + j, 0), 0)))
            for j in range(n_band)]
    return pl.pallas_call(
        functools.partial(_attn_kernel, n_band=n_band, first_key_chunk=-w_chunks),
        out_shape=jax.ShapeDtypeStruct((B, T, D), BF16),
        grid_spec=pltpu.PrefetchScalarGridSpec(
            num_scalar_prefetch=1,
            grid=(B, nc),
            in_specs=[pl.BlockSpec((1, CHUNK, D), lambda b, c, s: (b, c, 0))] + band + band,
            out_specs=pl.BlockSpec((1, CHUNK, D), lambda b, c, s: (b, c, 0)),
        ),
        compiler_params=_params("arbitrary", "arbitrary"),
        name="attn_prompt",
    )(sinks, q, *([kd] * n_band), *([vd] * n_band))


def _attn_sample(sinks, q, kd, vd):
    B, T, D = q.shape
    NK, KW = kd.shape[1:]
    return pl.pallas_call(
        functools.partial(_attn_kernel, n_band=1, first_key_chunk=None),
        out_shape=jax.ShapeDtypeStruct((B, T, D), BF16),
        grid_spec=pltpu.PrefetchScalarGridSpec(
            num_scalar_prefetch=1,
            grid=(B, 1),
            in_specs=[pl.BlockSpec((1, T, D), lambda b, c, s: (b, 0, 0)),
                      pl.BlockSpec((1, NK, KW), lambda b, c, s: (b, 0, 0)),
                      pl.BlockSpec((1, NK, KW), lambda b, c, s: (b, 0, 0))],
            out_specs=pl.BlockSpec((1, T, D), lambda b, c, s: (b, 0, 0)),
        ),
        compiler_params=_params("arbitrary", "arbitrary"),
        name="attn_sample",
    )(sinks, q, kd, vd)


def _dispatch(logits):
    n = logits.shape[0]
    g_logits = logits[:, :N_GROUPS]
    g_sel = jnp.argmax(g_logits, -1).astype(jnp.int32)
    g_prob = jnp.take_along_axis(jax.nn.softmax(g_logits, -1), g_sel[:, None], -1)
    e_logits = logits[:, N_GROUPS:N_GROUPS + N_EXPERTS].reshape(n, N_GROUPS, EXPERTS_PER_GROUP)
    e_sel = jnp.take_along_axis(e_logits, g_sel[:, None, None], axis=1)[:, 0]
    top_v, top_i = lax.top_k(e_sel, TOP_K)
    gates = jax.nn.softmax(top_v, -1) * g_prob
    expert = g_sel[:, None] * EXPERTS_PER_GROUP + top_i.astype(jnp.int32)

    na = n * TOP_K
    eid = expert.reshape(na)
    onehot = (eid[:, None] == jnp.arange(N_EXPERTS, dtype=jnp.int32)[None, :]).astype(jnp.int32)
    csum = jnp.cumsum(onehot, axis=0)
    rank = jnp.take_along_axis(csum, eid[:, None], axis=1)[:, 0] - 1
    counts = csum[-1]
    padded = (counts + MOE_ROWS - 1) // MOE_ROWS * MOE_ROWS
    pend = jnp.cumsum(padded)
    pstart = pend - padded
    dest = pstart[eid] + rank
    n_blocks = -(-na // MOE_ROWS) + N_EXPERTS
    rows = n_blocks * MOE_ROWS
    tok = jnp.repeat(jnp.arange(n, dtype=jnp.int32), TOP_K)
    row_tok = jnp.full((rows,), n, jnp.int32).at[dest].set(tok)
    row_gate = jnp.zeros((rows,), F32).at[dest].set(gates.reshape(na))
    blk_start = jnp.arange(n_blocks, dtype=jnp.int32) * MOE_ROWS
    blk_expert = jnp.minimum(jnp.searchsorted(pend, blk_start, side='right'), N_EXPERTS - 1).astype(jnp.int32)
    n_used = (pend[-1] // MOE_ROWS).astype(jnp.int32).reshape(1)
    return row_tok, row_gate, blk_expert, n_used, dest.reshape(n, TOP_K)


def _moe_layer(xs, mods, w_router, b_router, w1, w3, w2, cfgs):
    hs, lgs = [], []
    for x, (sh, sc), (nb, tt) in zip(xs, mods, cfgs):
        h, lg = _router(x, sc, sh, w_router, b_router, nb, tt)
        hs.append(h)
        lgs.append(lg)
    h_all = jnp.concatenate(hs, axis=0)
    lg_all = jnp.concatenate(lgs, axis=0)
    row_tok, row_gate, blk_expert, n_used, dest = _dispatch(lg_all)
    xb = jnp.take(h_all, row_tok, axis=0, mode='fill', fill_value=0)
    yb = _moe_ffn(blk_expert, n_used, xb, w1, w3, w2, row_gate[:, None])
    outs = []
    row = 0
    for x in xs:
        NB, TT, D = x.shape
        d = dest[row:row + NB * TT]
        y2 = jnp.stack([jnp.take(yb, d[:, j], axis=0) for j in range(TOP_K)]).reshape(TOP_K, NB, TT, D)
        outs.append(y2)
        row += NB * TT
    return outs


def _dup_heads(t):
    B, T, _ = t.shape
    t4 = t.reshape(B, T, N_KV_HEADS, 1, HEAD)
    return jnp.broadcast_to(t4, (B, T, N_KV_HEADS, PAIR, HEAD)).reshape(B, T, N_KV_HEADS * LANES).astype(BF16)


def _pad_cols(w, n):
    return jnp.pad(w, ((0, 0), (0, n - w.shape[1])))


def _pad_rows(w, n):
    return jnp.pad(w, ((0, n - w.shape[0]), (0, 0)))


def kernel(x_prompt, x_sample, c_prompt, c_sample, state_wkv, state_shift, cache_k, cache_v, mod_w, mod_b, ln_g, ln_b, rw_mu, rw_wr, rw_wk, rw_wv, rw_w0, rw_w1, rw_w2, rw_a0, rw_a1, rw_a2, rw_g1, rw_g2, rw_kk, rw_ka, rw_rk, rw_lnx_g, rw_lnx_b, rw_wo, kv_mod_w, kv_mod_b, w_kv, at_wq, at_sinks, at_wo, moe_wg, moe_bg, moe_wr, moe_br, moe_w1, moe_w3, moe_w2):
    B, T, D = x_prompt.shape
    BS, TS, _ = x_sample.shape
    H = D // HEAD
    KVW = N_KV_HEADS * HEAD
    cfgs = [(1, min(PROMPT_ROWS, T)), (BS, TS)]
    chunk_len = [CHUNK, TS]

    c_all = jnp.concatenate([c_prompt, c_sample], axis=0)
    m_all = _mods(c_all, mod_w.reshape(DEPTH * 2, D, 3 * D), mod_b.reshape(DEPTH * 2, 1, 3 * D))
    kvm = _mods(c_all, kv_mod_w[None], kv_mod_b[None, None])[0]
    rows = [slice(0, B), slice(B, B + BS)]

    def mod(layer, sub, part, path):
        return m_all[layer * 2 + sub, rows[path], part * D:(part + 1) * D][:, None, :]

    xs = [x_prompt, x_sample]
    shift0 = [jnp.zeros((B, 1, D), F32), state_shift[0][:, None, :]]
    wkv0 = [jnp.zeros((B, H, HEAD, HEAD), F32), state_wkv[0]]

    mu = rw_mu[0]
    mu_rkv = jnp.stack([mu[0], mu[2], mu[3]])[:, None, :]
    mu_lora = jnp.stack([mu[1], mu[4], mu[5]])
    w_rkv = jnp.stack([rw_wr[0], rw_wk[0], rw_wv[0]]).astype(BF16)
    lw1 = _pad_cols(rw_w1[0], LANES).astype(BF16)
    lw2 = _pad_rows(rw_w2[0], LANES).astype(BF16)
    la1 = _pad_cols(rw_a1[0], LANES).astype(BF16)
    la2 = _pad_rows(rw_a2[0], LANES).astype(BF16)
    lg1 = rw_g1[0].astype(BF16)
    lg2 = rw_g2[0].astype(BF16)
    wo = rw_wo[0].astype(BF16)
    vec = lambda v: v.reshape(1, D)
    new_wkv, new_shift, x1 = [], [], []
    for p in range(2):
        nb, tt = cfgs[p]
        sh, sc, gt = mod(0, 0, 0, p), mod(0, 0, 1, p), mod(0, 0, 2, p)
        rkv = _rkv(xs[p], sc, sh, shift0[p], mu_rkv, w_rkv, nb, tt)
        lw, a, g, last = _lora(xs[p], sc, sh, shift0[p], mu_lora, lw1, lw2, vec(rw_w0[0]), la1, la2, vec(rw_a0[0]),
                               lg1, lg2, nb, tt)
        z, s_fin = _wkv(rkv, lw, a, g, vec(rw_kk[0]), vec(rw_ka[0]), vec(rw_rk[0]), vec(rw_lnx_g[0]),
                        vec(rw_lnx_b[0]), wkv0[p], chunk_len[p])
        x1.append(_outproj_ln(z, xs[p], gt, ln_g[0, 0].reshape(1, 1, D), ln_b[0, 0].reshape(1, 1, D), wo, nb, tt))
        new_wkv.append(s_fin[None])
        new_shift.append(last.reshape(1, -1, D))

    def moe(layer, xin):
        w_router = _pad_cols(jnp.concatenate([moe_wg[layer], moe_wr[layer]], axis=1), LANES)
        b_router = _pad_cols(jnp.concatenate([moe_bg[layer], moe_br[layer]])[None, :], LANES)
        mods = [(mod(layer, 1, 0, p), mod(layer, 1, 1, p)) for p in range(2)]
        ys = _moe_layer(xin, mods, w_router, b_router, moe_w1[layer], moe_w3[layer], moe_w2[layer], cfgs)
        out = []
        for p in range(2):
            nb, tt = cfgs[p]
            out.append(_resid_ln(xin[p], ys[p], mod(layer, 1, 2, p), ln_g[layer, 1].reshape(1, 1, D),
                                 ln_b[layer, 1].reshape(1, 1, D), nb, tt))
        return out

    x2 = moe(0, x1)

    w_kv_b = w_kv.astype(BF16)
    kv = []
    for p in range(2):
        nb, tt = cfgs[p]
        kv_sh = kvm[rows[p], :D][:, None, :]
        kv_sc = kvm[rows[p], D:][:, None, :]
        kv.append(_modproj(x2[p], kv_sc, kv_sh, w_kv_b, nb, tt, F32))
    k_sh = [t[..., :KVW] for t in kv]
    v_sh = [t[..., KVW:] for t in kv]

    wq = at_wq[0].astype(BF16)
    wo1 = at_wo[0].astype(BF16)
    sinks = at_sinks[0].astype(F32)
    x3 = []
    for p in range(2):
        nb, tt = cfgs[p]
        sh, sc, gt = mod(1, 0, 0, p), mod(1, 0, 1, p), mod(1, 0, 2, p)
        q = _modproj(x2[p], sc, sh, wq, nb, tt, F32)
        if p == 0:
            o = _attn_prompt(sinks, q, _dup_heads(k_sh[p]), _dup_heads(v_sh[p]))
        else:
            k_all = jnp.concatenate([cache_k.reshape(BS, WINDOW, KVW), k_sh[p]], axis=1)
            v_all = jnp.concatenate([cache_v.reshape(BS, WINDOW, KVW), v_sh[p]], axis=1)
            o = _attn_sample(sinks, q, _dup_heads(k_all), _dup_heads(v_all))
        x3.append(_outproj_ln(o, x2[p], gt, ln_g[1, 0].reshape(1, 1, D), ln_b[1, 0].reshape(1, 1, D), wo1, nb, tt))

    x4 = moe(1, x3)

    k_p = k_sh[0][:, -WINDOW:].reshape(B, WINDOW, N_KV_HEADS, HEAD)
    v_p = v_sh[0][:, -WINDOW:].reshape(B, WINDOW, N_KV_HEADS, HEAD)
    k_s = k_sh[1].reshape(BS, TS, N_KV_HEADS, HEAD)
    v_s = v_sh[1].reshape(BS, TS, N_KV_HEADS, HEAD)
    return (x4[0], x4[1], new_wkv[0], new_shift[0], k_p, v_p, new_wkv[1], new_shift[1], k_s, v_s)
```

```python
import functools

import jax
import jax.numpy as jnp
from jax import lax
from jax.experimental import pallas as pl
from jax.experimental.pallas import tpu as pltpu

F32 = jnp.float32
BF16 = jnp.bfloat16

HEAD = 64
N_KV_HEADS = 4
WINDOW = 128
CHUNK = 64
N_GROUPS = 4
EXPERTS_PER_GROUP = 8
N_EXPERTS = N_GROUPS * EXPERTS_PER_GROUP
TOP_K = 2
DEPTH = 2
ALPHA = (2.0 * DEPTH) ** 0.25
LN_EPS = 1e-5
GN_EPS = 64e-5
NEG_INF = -1e30

LANES = 128
PAIR = LANES // HEAD

PROMPT_ROWS = 512
MOE_ROWS = 256
MODS_TN = 1024
WKV_PASSES = 1
WKV_UNROLL = 8
ATTN_HEADS_PER_STAGE = 16
VMEM_LIMIT = 56 * 1024 * 1024

NN = (((1,), (0,)), ((), ()))
NT = (((1,), (1,)), ((), ()))
TN = (((0,), (0,)), ((), ()))


def _mm(a, b, dims=NN, passes=1):
    if passes == 6:
        return lax.dot_general(a, b, dims, precision=lax.Precision.HIGHEST, preferred_element_type=F32)
    ah = a.astype(BF16)
    bh = b.astype(BF16)
    out = lax.dot_general(ah, bh, dims, preferred_element_type=F32)
    if passes == 3:
        al = (a - ah.astype(F32)).astype(BF16)
        bl = (b - bh.astype(F32)).astype(BF16)
        out = out + lax.dot_general(ah, bl, dims, preferred_element_type=F32)
        out = out + lax.dot_general(al, bh, dims, preferred_element_type=F32)
    return out


def _bf16_terms(x, terms):
    out = []
    for _ in range(terms):
        t = x.astype(BF16)
        out.append(t)
        x = x - t.astype(F32)
    return out


def _mm_mask(a, b, dims=NN, split="lhs", terms=2):
    if split == "lhs":
        bb = b.astype(BF16)
        parts = [lax.dot_general(t, bb, dims, preferred_element_type=F32) for t in _bf16_terms(a, terms)]
    else:
        aa = a.astype(BF16)
        parts = [lax.dot_general(aa, t, dims, preferred_element_type=F32) for t in _bf16_terms(b, terms)]
    return functools.reduce(lambda p, q: p + q, parts)


def _sigmoid(x):
    return 1.0 / (1.0 + jnp.exp(-x))


def _layer_norm(v, g, b):
    mu = jnp.mean(v, axis=-1, keepdims=True)
    d = v - mu
    var = jnp.mean(d * d, axis=-1, keepdims=True)
    return d * lax.rsqrt(var + LN_EPS) * g + b


def _params(*sem):
    return pltpu.CompilerParams(dimension_semantics=sem, vmem_limit_bytes=VMEM_LIMIT)


def _mods_kernel(c_ref, w_ref, b_ref, o_ref):
    c = c_ref[...]
    a = (c * _sigmoid(c)).astype(BF16)
    o_ref[0] = jnp.dot(a, w_ref[0].astype(BF16), preferred_element_type=F32) + b_ref[0]


def _mods(c, w, b):
    g, d, n = w.shape
    m = c.shape[0]
    tn = min(MODS_TN, n)
    assert n % tn == 0
    return pl.pallas_call(
        _mods_kernel,
        out_shape=jax.ShapeDtypeStruct((g, m, n), F32),
        grid=(g, n // tn),
        in_specs=[
            pl.BlockSpec((m, d), lambda i, j: (0, 0)),
            pl.BlockSpec((1, d, tn), lambda i, j: (i, 0, j)),
            pl.BlockSpec((1, 1, tn), lambda i, j: (i, 0, j)),
        ],
        out_specs=pl.BlockSpec((1, m, tn), lambda i, j: (i, 0, j)),
        compiler_params=_params("arbitrary", "arbitrary"),
        name="mods",
    )(c, w, b)


def _modulate(x_ref, sc_ref, sh_ref):
    nb, tt, d = x_ref.shape
    h = x_ref[...] * (1.0 + sc_ref[...]) + sh_ref[...]
    return h.reshape(nb * tt, d)


def _token_shift(h, prev_scr, prev0_ref, tt, t_axis):
    rows, d = h.shape
    nb = rows // tt

    @pl.when(pl.program_id(t_axis) == 0)
    def _():
        prev_scr[...] = prev0_ref[...]

    prev = jnp.broadcast_to(prev_scr[...], (nb, tt, d)).reshape(rows, d)
    rolled = pltpu.roll(h, 1, 0)
    row = lax.broadcasted_iota(jnp.int32, (rows, 1), 0)
    first = (row & (tt - 1)) == 0
    hp = jnp.where(first, prev, rolled)
    prev_scr[...] = h.reshape(nb, tt, d)[:, tt - 1:tt, :]
    return hp


def _rkv_kernel(x_ref, sc_ref, sh_ref, prev0_ref, mu_ref, w_ref, o_ref, prev_scr):
    nb, tt, d = x_ref.shape
    h = _modulate(x_ref, sc_ref, sh_ref)
    hp = _token_shift(h, prev_scr, prev0_ref, tt, 2)
    xin = h + (hp - h) * mu_ref[0]
    out = jnp.dot(xin.astype(BF16), w_ref[0], preferred_element_type=F32)
    o_ref[0] = out.reshape(nb, tt, out.shape[-1])


def _rkv(x, sc, sh, prev0, mu3, w3, nb, tt):
    NB, TT, D = x.shape
    seq = lambda j, b, t: (b, t, 0)
    per = lambda j, b, t: (b, 0, 0)
    return pl.pallas_call(
        _rkv_kernel,
        out_shape=jax.ShapeDtypeStruct((3, NB, TT, D), F32),
        grid=(3, NB // nb, TT // tt),
        in_specs=[
            pl.BlockSpec((nb, tt, D), seq),
            pl.BlockSpec((nb, 1, D), per),
            pl.BlockSpec((nb, 1, D), per),
            pl.BlockSpec((nb, 1, D), per),
            pl.BlockSpec((1, 1, D), lambda j, b, t: (j, 0, 0)),
            pl.BlockSpec((1, D, D), lambda j, b, t: (j, 0, 0)),
        ],
        out_specs=pl.BlockSpec((1, nb, tt, D), lambda j, b, t: (j, b, t, 0)),
        scratch_shapes=[pltpu.VMEM((nb, 1, D), F32)],
        compiler_params=_params("arbitrary", "arbitrary", "arbitrary"),
        name="tmix_rkv",
    )(x, sc, sh, prev0, mu3, w3)


def _lora_kernel(x_ref, sc_ref, sh_ref, prev0_ref, mu_ref, w1_ref, w2_ref, w0_ref, a1_ref, a2_ref, a0_ref,
                 g1_ref, g2_ref, lw_ref, a_ref, g_ref, last_ref, prev_scr):
    nb, tt, d = x_ref.shape
    h = _modulate(x_ref, sc_ref, sh_ref)
    hp = _token_shift(h, prev_scr, prev0_ref, tt, 1)
    dx = hp - h
    last_ref[...] = h.reshape(nb, tt, d)[:, tt - 1:tt, :]

    xw = (h + dx * mu_ref[0:1, :]).astype(BF16)
    zw = jnp.tanh(jnp.dot(xw, w1_ref[...], preferred_element_type=F32))
    zw = jnp.dot(zw.astype(BF16), w2_ref[...], preferred_element_type=F32) + w0_ref[...]
    sp = jnp.maximum(-zw, 0.0) + jnp.log1p(jnp.exp(-jnp.abs(zw)))
    lw_ref[...] = (-jnp.exp(-sp - 0.5)).reshape(nb, tt, d)

    xa = (h + dx * mu_ref[1:2, :]).astype(BF16)
    za = jnp.dot(xa, a1_ref[...], preferred_element_type=F32)
    za = jnp.dot(za.astype(BF16), a2_ref[...], preferred_element_type=F32) + a0_ref[...]
    a_ref[...] = _sigmoid(za).reshape(nb, tt, d)

    xg = (h + dx * mu_ref[2:3, :]).astype(BF16)
    zg = _sigmoid(jnp.dot(xg, g1_ref[...], preferred_element_type=F32))
    g_ref[...] = jnp.dot(zg.astype(BF16), g2_ref[...], preferred_element_type=F32).reshape(nb, tt, d)


def _lora(x, sc, sh, prev0, mu3, w1, w2, w0, a1, a2, a0, g1, g2, nb, tt):
    NB, TT, D = x.shape
    seq = lambda b, t: (b, t, 0)
    per = lambda b, t: (b, 0, 0)
    full = lambda a: pl.BlockSpec(a.shape, lambda b, t: (0,) * a.ndim)
    big = jax.ShapeDtypeStruct((NB, TT, D), F32)
    return pl.pallas_call(
        _lora_kernel,
        out_shape=(big, big, big, jax.ShapeDtypeStruct((NB, 1, D), F32)),
        grid=(NB // nb, TT // tt),
        in_specs=[
            pl.BlockSpec((nb, tt, D), seq),
            pl.BlockSpec((nb, 1, D), per),
            pl.BlockSpec((nb, 1, D), per),
            pl.BlockSpec((nb, 1, D), per),
            full(mu3), full(w1), full(w2), full(w0), full(a1), full(a2), full(a0), full(g1), full(g2),
        ],
        out_specs=(pl.BlockSpec((nb, tt, D), seq),) * 3 + (pl.BlockSpec((nb, 1, D), per),),
        scratch_shapes=[pltpu.VMEM((nb, 1, D), F32)],
        compiler_params=_params("arbitrary", "arbitrary"),
        name="tmix_lora",
    )(x, sc, sh, prev0, mu3, w1, w2, w0, a1, a2, a0, g1, g2)


def _wkv_kernel(r_ref, k_ref, v_ref, lw_ref, a_ref, g_ref, kkc_ref, kac_ref, rk_ref, lng_ref, lnb_ref, s0_ref,
                z_ref, sfin_ref, s_scr, *, passes, unroll):
    L = r_ref.shape[2]
    n_heads = s_scr.shape[0]
    unroll = min(unroll, n_heads // PAIR)
    c = pl.program_id(1)

    @pl.when(c == 0)
    def _():
        s_scr[...] = jnp.zeros(s_scr.shape, F32)
        for h in range(n_heads):
            o = (h % PAIR) * HEAD
            s_scr[h, o:o + HEAD, o:o + HEAD] = s0_ref[0, h]

    lane = lax.broadcasted_iota(jnp.int32, (1, LANES), 1)
    ri = lax.broadcasted_iota(jnp.int32, (L, L), 0)
    ci = lax.broadcasted_iota(jnp.int32, (L, L), 1)
    strict = ri > ci
    incl = ri >= ci
    tril = jnp.where(incl, 1.0, 0.0).astype(F32)
    bi = lax.broadcasted_iota(jnp.int32, (LANES, LANES), 0) // HEAD
    bj = lax.broadcasted_iota(jnp.int32, (LANES, LANES), 1) // HEAD
    head_ones = jnp.where(bi == bj, 1.0, 0.0).astype(F32)

    def group_body(gi, carry):
        pairs = [gi * unroll + u for u in range(unroll)]
        sls = [pl.ds(pl.multiple_of(p * LANES, LANES), LANES) for p in pairs]
        heads = [(i, hh) for i in range(unroll) for hh in range(PAIR)]
        masks = [lane < HEAD, lane >= HEAD]
        s0 = [s_scr[PAIR * pairs[i] + hh] for i, hh in heads]

        rp = [r_ref[0, 0, :, sl] for sl in sls]
        kp = [k_ref[0, 0, :, sl] for sl in sls]
        vp = [v_ref[0, 0, :, sl] for sl in sls]
        lwp = [lw_ref[0, :, sl] for sl in sls]
        ap = [a_ref[0, :, sl] for sl in sls]
        cum = [_mm_mask(tril, t, split="rhs", terms=3) for t in lwp]
        kk0 = [k * kkc_ref[:, sl] for k, sl in zip(kp, sls)]
        n2 = [_mm_mask(t * t, head_ones) for t in kk0]
        kkp = [t * lax.rsqrt(jnp.maximum(n, 1e-24)) for t, n in zip(kk0, n2)]
        khp = [k * (1.0 + (a - 1.0) * kac_ref[:, sl]) for k, a, sl in zip(kp, ap, sls)]
        c_incl = [jnp.exp(t) for t in cum]
        c_inv = [jnp.exp(-t) for t in cum]
        kt = [kk * jnp.exp(t - lw) for kk, t, lw in zip(kkp, cum, lwp)]
        rt = [r * ci_ for r, ci_ in zip(rp, c_incl)]
        kti = [kh * cv for kh, cv in zip(khp, c_inv)]
        bti = [kk * a * cv for kk, a, cv in zip(kkp, ap, c_inv)]
        lhs = [jnp.concatenate([a_, b_], axis=0) for a_, b_ in zip(kt, rt)]

        lhs_m = [jnp.where(masks[hh], lhs[i], 0.0) for i, hh in heads]
        gk = [_mm(lhs_m[j], kti[i], NT, passes) for j, (i, hh) in enumerate(heads)]
        gb = [_mm(lhs_m[j], bti[i], NT, passes) for j, (i, hh) in enumerate(heads)]
        p0 = [_mm(lhs[i], s0[j], NT, passes) for j, (i, hh) in enumerate(heads)]
        a_kk = [jnp.where(strict, t[:L], 0.0) for t in gk]
        a_rk = [jnp.where(incl, t[L:], 0.0) for t in gk]
        a_rb = [jnp.where(incl, t[L:], 0.0) for t in gb]
        pw = [jnp.where(strict, -t[:L], 0.0) for t in gb]
        x = [p0[j][:L] + _mm(a_kk[j], vp[i], NN, passes) for j, (i, hh) in enumerate(heads)]
        u = [x_ + _mm(p_, x_, NN, passes) for p_, x_ in zip(pw, x)]
        n = 1
        while n * 2 < L:
            pw = [_mm(p_, p_, NN, passes) for p_ in pw]
            u = [u_ + _mm(p_, u_, NN, passes) for p_, u_ in zip(pw, u)]
            n *= 2
        y = [p0[j][L:] + _mm(a_rk[j], vp[i], NN, passes) - _mm(a_rb[j], u[j], NN, passes)
             for j, (i, hh) in enumerate(heads)]
        vu = [jnp.where(masks[hh], jnp.concatenate([vp[i], u[j]], axis=0), 0.0) for j, (i, hh) in enumerate(heads)]
        kb = [jnp.where(masks[hh], jnp.concatenate([kti[i], -bti[i]], axis=0), 0.0) for i, hh in heads]
        ds = [_mm(vu_, kb_, TN, passes) for vu_, kb_ in zip(vu, kb)]
        s_new = [(s0[j] + ds[j]) * c_incl[i][L - 1:L, :] for j, (i, hh) in enumerate(heads)]

        yp = [jnp.where(masks[0], y[PAIR * i], y[PAIR * i + 1]) for i in range(unroll)]
        mean = [_mm_mask(t, head_ones) * (1.0 / HEAD) for t in yp]
        dlt = [t - m_ for t, m_ in zip(yp, mean)]
        var = [_mm_mask(t * t, head_ones) * (1.0 / HEAD) for t in dlt]
        bonus = [_mm_mask(r * kh * rk_ref[:, sl], head_ones) * v
                 for r, kh, v, sl in zip(rp, khp, vp, sls)]
        for i, sl in enumerate(sls):
            yn = dlt[i] * lax.rsqrt(var[i] + GN_EPS) * lng_ref[:, sl] + lnb_ref[:, sl]
            z_ref[0, :, sl] = ((yn + bonus[i]) * g_ref[0, :, sl]).astype(z_ref.dtype)
        for j, (i, hh) in enumerate(heads):
            s_scr[PAIR * pairs[i] + hh] = s_new[j]
        return carry

    assert (n_heads // PAIR) % unroll == 0
    lax.fori_loop(0, n_heads // PAIR // unroll, group_body, 0)

    @pl.when(c == pl.num_programs(1) - 1)
    def _():
        for h in range(n_heads):
            o = (h % PAIR) * HEAD
            sfin_ref[0, h] = s_scr[h, o:o + HEAD, o:o + HEAD]


def _wkv(rkv, lw, a, g, kkc, kac, rk, lng, lnb, s0, L, passes=WKV_PASSES, unroll=WKV_UNROLL):
    _, NB, TT, D = rkv.shape
    H = D // HEAD
    seq = lambda b, t: (b, t, 0)
    vec = pl.BlockSpec((1, D), lambda b, t: (0, 0))
    st = pl.BlockSpec((1, H, HEAD, HEAD), lambda b, t: (b, 0, 0, 0))
    return pl.pallas_call(
        functools.partial(_wkv_kernel, passes=passes, unroll=unroll),
        out_shape=(jax.ShapeDtypeStruct((NB, TT, D), BF16), jax.ShapeDtypeStruct((NB, H, HEAD, HEAD), F32)),
        grid=(NB, TT // L),
        in_specs=[
            pl.BlockSpec((1, 1, L, D), lambda b, t: (0, b, t, 0)),
            pl.BlockSpec((1, 1, L, D), lambda b, t: (1, b, t, 0)),
            pl.BlockSpec((1, 1, L, D), lambda b, t: (2, b, t, 0)),
            pl.BlockSpec((1, L, D), seq),
            pl.BlockSpec((1, L, D), seq),
            pl.BlockSpec((1, L, D), seq),
            vec, vec, vec, vec, vec, st,
        ],
        out_specs=(pl.BlockSpec((1, L, D), seq), st),
        scratch_shapes=[pltpu.VMEM((H, LANES, LANES), F32)],
        compiler_params=_params("arbitrary", "arbitrary"),
        name="wkv",
    )(rkv, rkv, rkv, lw, a, g, kkc, kac, rk, lng, lnb, s0)


def _outproj_ln_kernel(a_ref, x_ref, gate_ref, lng_ref, lnb_ref, w_ref, o_ref):
    nb, tt, d = x_ref.shape
    a = a_ref[...].reshape(nb * tt, a_ref.shape[-1])
    y = jnp.dot(a, w_ref[...], preferred_element_type=F32).reshape(nb, tt, d)
    v = ALPHA * x_ref[...] + (1.0 + gate_ref[...]) * y
    o_ref[...] = _layer_norm(v, lng_ref[...], lnb_ref[...])


def _outproj_ln(a, x, gate, lng, lnb, w, nb, tt):
    NB, TT, D = x.shape
    seq = lambda b, t: (b, t, 0)
    per = lambda b, t: (b, 0, 0)
    vec = pl.BlockSpec((1, 1, D), lambda b, t: (0, 0, 0))
    return pl.pallas_call(
        _outproj_ln_kernel,
        out_shape=jax.ShapeDtypeStruct((NB, TT, D), F32),
        grid=(NB // nb, TT // tt),
        in_specs=[
            pl.BlockSpec((nb, tt, a.shape[-1]), seq),
            pl.BlockSpec((nb, tt, D), seq),
            pl.BlockSpec((nb, 1, D), per),
            vec, vec,
            pl.BlockSpec(w.shape, lambda b, t: (0, 0)),
        ],
        out_specs=pl.BlockSpec((nb, tt, D), seq),
        compiler_params=_params("arbitrary", "arbitrary"),
        name="outproj_ln",
    )(a, x, gate, lng, lnb, w)


def _resid_ln_kernel(x_ref, y_ref, gate_ref, lng_ref, lnb_ref, o_ref):
    y = y_ref[0] + y_ref[1]
    v = ALPHA * x_ref[...] + (1.0 + gate_ref[...]) * y
    o_ref[...] = _layer_norm(v, lng_ref[...], lnb_ref[...])


def _resid_ln(x, y2, gate, lng, lnb, nb, tt):
    NB, TT, D = x.shape
    seq = lambda b, t: (b, t, 0)
    per = lambda b, t: (b, 0, 0)
    vec = pl.BlockSpec((1, 1, D), lambda b, t: (0, 0, 0))
    return pl.pallas_call(
        _resid_ln_kernel,
        out_shape=jax.ShapeDtypeStruct((NB, TT, D), F32),
        grid=(NB // nb, TT // tt),
        in_specs=[
            pl.BlockSpec((nb, tt, D), seq),
            pl.BlockSpec((TOP_K, nb, tt, D), lambda b, t: (0, b, t, 0)),
            pl.BlockSpec((nb, 1, D), per),
            vec, vec,
        ],
        out_specs=pl.BlockSpec((nb, tt, D), seq),
        compiler_params=_params("arbitrary", "arbitrary"),
        name="resid_ln",
    )(x, y2, gate, lng, lnb)


def _modproj_kernel(x_ref, sc_ref, sh_ref, w_ref, o_ref):
    nb, tt, d = x_ref.shape
    h = _modulate(x_ref, sc_ref, sh_ref)
    out = jnp.dot(h.astype(BF16), w_ref[...], preferred_element_type=F32)
    o_ref[...] = out.reshape(nb, tt, out.shape[-1]).astype(o_ref.dtype)


def _modproj(x, sc, sh, w, nb, tt, out_dtype):
    NB, TT, D = x.shape
    N = w.shape[1]
    seq = lambda b, t: (b, t, 0)
    per = lambda b, t: (b, 0, 0)
    return pl.pallas_call(
        _modproj_kernel,
        out_shape=jax.ShapeDtypeStruct((NB, TT, N), out_dtype),
        grid=(NB // nb, TT // tt),
        in_specs=[
            pl.BlockSpec((nb, tt, D), seq),
            pl.BlockSpec((nb, 1, D), per),
            pl.BlockSpec((nb, 1, D), per),
            pl.BlockSpec(w.shape, lambda b, t: (0, 0)),
        ],
        out_specs=pl.BlockSpec((nb, tt, N), seq),
        compiler_params=_params("arbitrary", "arbitrary"),
        name="modproj",
    )(x, sc, sh, w)


def _router_kernel(x_ref, sc_ref, sh_ref, w_ref, b_ref, h_ref, lg_ref):
    nb, tt, d = x_ref.shape
    h = _modulate(x_ref, sc_ref, sh_ref)
    h_ref[...] = h.astype(BF16)
    lg_ref[...] = _mm(h, w_ref[...], NN, passes=3) + b_ref[...]


def _router(x, sc, sh, w, b, nb, tt):
    NB, TT, D = x.shape
    steps_t = TT // tt
    seq = lambda b_, t: (b_, t, 0)
    per = lambda b_, t: (b_, 0, 0)
    return pl.pallas_call(
        _router_kernel,
        out_shape=(jax.ShapeDtypeStruct((NB * TT, D), BF16), jax.ShapeDtypeStruct((NB * TT, LANES), F32)),
        grid=(NB // nb, steps_t),
        in_specs=[
            pl.BlockSpec((nb, tt, D), seq),
            pl.BlockSpec((nb, 1, D), per),
            pl.BlockSpec((nb, 1, D), per),
            pl.BlockSpec(w.shape, lambda b_, t: (0, 0)),
            pl.BlockSpec(b.shape, lambda b_, t: (0, 0)),
        ],
        out_specs=(pl.BlockSpec((nb * tt, D), lambda b_, t: (b_ * steps_t + t, 0)),
                   pl.BlockSpec((nb * tt, LANES), lambda b_, t: (b_ * steps_t + t, 0))),
        compiler_params=_params("arbitrary", "arbitrary"),
        name="router",
    )(x, sc, sh, w, b)


def _moe_kernel(be_ref, used_ref, x_ref, w1_ref, w3_ref, w2_ref, gate_ref, o_ref, w1s, w3s, w2s):
    i = pl.program_id(0)
    e = be_ref[i]
    e_prev = be_ref[jnp.maximum(i - 1, 0)]

    @pl.when((i == 0) | (e != e_prev))
    def _():
        w1s[...] = w1_ref[0].astype(BF16)
        w3s[...] = w3_ref[0].astype(BF16)
        w2s[...] = w2_ref[0].astype(BF16)

    @pl.when(i < used_ref[0])
    def _():
        x = x_ref[...]
        h1 = jnp.dot(x, w1s[...], preferred_element_type=F32)
        h3 = jnp.dot(x, w3s[...], preferred_element_type=F32)
        act = (h1 * _sigmoid(h1) * h3).astype(BF16)
        y = jnp.dot(act, w2s[...], preferred_element_type=F32)
        o_ref[...] = y * gate_ref[...]

    @pl.when(i >= used_ref[0])
    def _():
        o_ref[...] = jnp.zeros(o_ref.shape, F32)


def _moe_ffn(blk_expert, n_used, xb, w1, w3, w2, row_gate):
    rows, D = xb.shape
    E, _, DE = w1.shape
    n_blocks = rows // MOE_ROWS
    return pl.pallas_call(
        _moe_kernel,
        out_shape=jax.ShapeDtypeStruct((rows, D), F32),
        grid_spec=pltpu.PrefetchScalarGridSpec(
            num_scalar_prefetch=2,
            grid=(n_blocks,),
            in_specs=[
                pl.BlockSpec((MOE_ROWS, D), lambda i, be, nu: (i, 0)),
                pl.BlockSpec((1, D, DE), lambda i, be, nu: (be[i], 0, 0)),
                pl.BlockSpec((1, D, DE), lambda i, be, nu: (be[i], 0, 0)),
                pl.BlockSpec((1, DE, D), lambda i, be, nu: (be[i], 0, 0)),
                pl.BlockSpec((MOE_ROWS, 1), lambda i, be, nu: (i, 0)),
            ],
            out_specs=pl.BlockSpec((MOE_ROWS, D), lambda i, be, nu: (i, 0)),
            scratch_shapes=[pltpu.VMEM((D, DE), BF16), pltpu.VMEM((D, DE), BF16), pltpu.VMEM((DE, D), BF16)],
        ),
        compiler_params=_params("arbitrary"),
        name="moe_ffn",
    )(blk_expert, n_used, xb, w1, w3, w2, row_gate)


def _attn_kernel(sinks_ref, q_ref, *refs, n_band, first_key_chunk):
    k_refs = refs[:n_band]
    v_refs = refs[n_band:2 * n_band]
    o_ref = refs[2 * n_band]
    tq = q_ref.shape[1]
    kd = jnp.concatenate([r[0] for r in k_refs], axis=0) if n_band > 1 else k_refs[0][0]
    vd = jnp.concatenate([r[0] for r in v_refs], axis=0) if n_band > 1 else v_refs[0][0]
    nk = kd.shape[0]
    qi = lax.broadcasted_iota(jnp.int32, (tq, nk), 0)
    kj = lax.broadcasted_iota(jnp.int32, (tq, nk), 1)
    dist = jnp.abs((nk - tq) + qi - kj).astype(F32)
    if first_key_chunk is not None:
        k_pos = (pl.program_id(1) + first_key_chunk) * tq + kj
        valid = k_pos >= 0
    lane = lax.broadcasted_iota(jnp.int32, (1, LANES), 1)
    n_q_heads = q_ref.shape[2] // HEAD
    q_per_kv = n_q_heads // N_KV_HEADS
    scale = HEAD ** -0.5
    k_half, v_half = [], []
    for g in range(N_KV_HEADS):
        kg = kd[:, g * LANES:(g + 1) * LANES]
        vg = vd[:, g * LANES:(g + 1) * LANES]
        zero = jnp.zeros_like(kg)
        k_half.append([jnp.where(lane < HEAD, kg, zero), jnp.where(lane >= HEAD, kg, zero)])
        v_half.append([jnp.where(lane < HEAD, vg, zero), jnp.where(lane >= HEAD, vg, zero)])
    per_stage = min(ATTN_HEADS_PER_STAGE, n_q_heads)
    for h0 in range(0, n_q_heads, per_stage):
        hs = list(range(h0, h0 + per_stage))
        qp = {h // PAIR: q_ref[0, :, (h // PAIR) * LANES:(h // PAIR + 1) * LANES].astype(BF16) for h in hs}
        s = [lax.dot_general(qp[h // PAIR], k_half[h // q_per_kv][h % PAIR], NT, preferred_element_type=F32)
             for h in hs]
        s = [t * scale - (2.0 ** (-8.0 * (h + 1) / n_q_heads)) * dist for t, h in zip(s, hs)]
        if first_key_chunk is not None:
            s = [jnp.where(valid, t, NEG_INF) for t in s]
        mx = [jnp.maximum(jnp.max(t, axis=-1, keepdims=True), sinks_ref[h]) for t, h in zip(s, hs)]
        pr = [jnp.exp(t - m) for t, m in zip(s, mx)]
        den = [jnp.sum(t, axis=-1, keepdims=True) + jnp.exp(sinks_ref[h] - m) for t, m, h in zip(pr, mx, hs)]
        pr = [(t / d).astype(BF16) for t, d in zip(pr, den)]
        o = [jnp.dot(t, v_half[h // q_per_kv][h % PAIR], preferred_element_type=F32) for t, h in zip(pr, hs)]
        for j in range(0, len(hs), PAIR):
            pair = hs[j] // PAIR
            o_ref[0, :, pair * LANES:(pair + 1) * LANES] = (o[j] + o[j + 1]).astype(o_ref.dtype)


def _attn_prompt(sinks, q, kd, vd):
    B, T, D = q.shape
    nc = T // CHUNK
    w_chunks = WINDOW // CHUNK
    n_band = w_chunks + 1
    KW = kd.shape[-1]
    band = [pl.BlockSpec((1, CHUNK, KW), (lambda b, c, s, j=j: (b, jnp.maximum(c - w_chunks + j, 0), 0)))
            for j in range(n_band)]
    return pl.pallas_call(
        functools.partial(_attn_kernel, n_band=n_band, first_key_chunk=-w_chunks),
        out_shape=jax.ShapeDtypeStruct((B, T, D), BF16),
        grid_spec=pltpu.PrefetchScalarGridSpec(
            num_scalar_prefetch=1,
            grid=(B, nc),
            in_specs=[pl.BlockSpec((1, CHUNK, D), lambda b, c, s: (b, c, 0))] + band + band,
            out_specs=pl.BlockSpec((1, CHUNK, D), lambda b, c, s: (b, c, 0)),
        ),
        compiler_params=_params("arbitrary", "arbitrary"),
        name="attn_prompt",
    )(sinks, q, *([kd] * n_band), *([vd] * n_band))


def _attn_sample(sinks, q, kd, vd):
    B, T, D = q.shape
    NK, KW = kd.shape[1:]
    return pl.pallas_call(
        functools.partial(_attn_kernel, n_band=1, first_key_chunk=None),
        out_shape=jax.ShapeDtypeStruct((B, T, D), BF16),
        grid_spec=pltpu.PrefetchScalarGridSpec(
            num_scalar_prefetch=1,
            grid=(B, 1),
            in_specs=[pl.BlockSpec((1, T, D), lambda b, c, s: (b, 0, 0)),
                      pl.BlockSpec((1, NK, KW), lambda b, c, s: (b, 0, 0)),
                      pl.BlockSpec((1, NK, KW), lambda b, c, s: (b, 0, 0))],
            out_specs=pl.BlockSpec((1, T, D), lambda b, c, s: (b, 0, 0)),
        ),
        compiler_params=_params("arbitrary", "arbitrary"),
        name="attn_sample",
    )(sinks, q, kd, vd)


def _dispatch(logits):
    n = logits.shape[0]
    g_logits = logits[:, :N_GROUPS]
    g_sel = jnp.argmax(g_logits, -1).astype(jnp.int32)
    g_prob = jnp.take_along_axis(jax.nn.softmax(g_logits, -1), g_sel[:, None], -1)
    e_logits = logits[:, N_GROUPS:N_GROUPS + N_EXPERTS].reshape(n, N_GROUPS, EXPERTS_PER_GROUP)
    e_sel = jnp.take_along_axis(e_logits, g_sel[:, None, None], axis=1)[:, 0]
    top_v, top_i = lax.top_k(e_sel, TOP_K)
    gates = jax.nn.softmax(top_v, -1) * g_prob
    expert = g_sel[:, None] * EXPERTS_PER_GROUP + top_i.astype(jnp.int32)

    na = n * TOP_K
    eid = expert.reshape(na)
    onehot = (eid[:, None] == jnp.arange(N_EXPERTS, dtype=jnp.int32)[None, :]).astype(jnp.int32)
    csum = jnp.cumsum(onehot, axis=0)
    rank = jnp.take_along_axis(csum, eid[:, None], axis=1)[:, 0] - 1
    counts = csum[-1]
    padded = (counts + MOE_ROWS - 1) // MOE_ROWS * MOE_ROWS
    pend = jnp.cumsum(padded)
    pstart = pend - padded
    dest = pstart[eid] + rank
    n_blocks = -(-na // MOE_ROWS) + N_EXPERTS
    rows = n_blocks * MOE_ROWS
    tok = jnp.repeat(jnp.arange(n, dtype=jnp.int32), TOP_K)
    row_tok = jnp.full((rows,), n, jnp.int32).at[dest].set(tok)
    row_gate = jnp.zeros((rows,), F32).at[dest].set(gates.reshape(na))
    blk_start = jnp.arange(n_blocks, dtype=jnp.int32) * MOE_ROWS
    blk_expert = jnp.minimum(jnp.searchsorted(pend, blk_start, side='right'), N_EXPERTS - 1).astype(jnp.int32)
    n_used = (pend[-1] // MOE_ROWS).astype(jnp.int32).reshape(1)
    return row_tok, row_gate, blk_expert, n_used, dest.reshape(n, TOP_K)


def _moe_layer(xs, mods, w_router, b_router, w1, w3, w2, cfgs):
    hs, lgs = [], []
    for x, (sh, sc), (nb, tt) in zip(xs, mods, cfgs):
        h, lg = _router(x, sc, sh, w_router, b_router, nb, tt)
        hs.append(h)
        lgs.append(lg)
    h_all = jnp.concatenate(hs, axis=0)
    lg_all = jnp.concatenate(lgs, axis=0)
    row_tok, row_gate, blk_expert, n_used, dest = _dispatch(lg_all)
    xb = jnp.take(h_all, row_tok, axis=0, mode='fill', fill_value=0)
    yb = _moe_ffn(blk_expert, n_used, xb, w1, w3, w2, row_gate[:, None])
    outs = []
    row = 0
    for x in xs:
        NB, TT, D = x.shape
        d = dest[row:row + NB * TT]
        y2 = jnp.stack([jnp.take(yb, d[:, j], axis=0) for j in range(TOP_K)]).reshape(TOP_K, NB, TT, D)
        outs.append(y2)
        row += NB * TT
    return outs


def _dup_heads(t):
    B, T, _ = t.shape
    t4 = t.reshape(B, T, N_KV_HEADS, 1, HEAD)
    return jnp.broadcast_to(t4, (B, T, N_KV_HEADS, PAIR, HEAD)).reshape(B, T, N_KV_HEADS * LANES).astype(BF16)


def _pad_cols(w, n):
    return jnp.pad(w, ((0, 0), (0, n - w.shape[1])))


def _pad_rows(w, n):
    return jnp.pad(w, ((0, n - w.shape[0]), (0, 0)))


def kernel(x_prompt, x_sample, c_prompt, c_sample, state_wkv, state_shift, cache_k, cache_v, mod_w, mod_b, ln_g, ln_b, rw_mu, rw_wr, rw_wk, rw_wv, rw_w0, rw_w1, rw_w2, rw_a0, rw_a1, rw_a2, rw_g1, rw_g2, rw_kk, rw_ka, rw_rk, rw_lnx_g, rw_lnx_b, rw_wo, kv_mod_w, kv_mod_b, w_kv, at_wq, at_sinks, at_wo, moe_wg, moe_bg, moe_wr, moe_br, moe_w1, moe_w3, moe_w2):
    B, T, D = x_prompt.shape
    BS, TS, _ = x_sample.shape
    H = D // HEAD
    KVW = N_KV_HEADS * HEAD
    cfgs = [(1, min(PROMPT_ROWS, T)), (BS, TS)]
    chunk_len = [CHUNK, TS]

    c_all = jnp.concatenate([c_prompt, c_sample], axis=0)
    m_all = _mods(c_all, mod_w.reshape(DEPTH * 2, D, 3 * D), mod_b.reshape(DEPTH * 2, 1, 3 * D))
    kvm = _mods(c_all, kv_mod_w[None], kv_mod_b[None, None])[0]
    rows = [slice(0, B), slice(B, B + BS)]

    def mod(layer, sub, part, path):
        return m_all[layer * 2 + sub, rows[path], part * D:(part + 1) * D][:, None, :]

    xs = [x_prompt, x_sample]
    shift0 = [jnp.zeros((B, 1, D), F32), state_shift[0][:, None, :]]
    wkv0 = [jnp.zeros((B, H, HEAD, HEAD), F32), state_wkv[0]]

    mu = rw_mu[0]
    mu_rkv = jnp.stack([mu[0], mu[2], mu[3]])[:, None, :]
    mu_lora = jnp.stack([mu[1], mu[4], mu[5]])
    w_rkv = jnp.stack([rw_wr[0], rw_wk[0], rw_wv[0]]).astype(BF16)
    lw1 = _pad_cols(rw_w1[0], LANES).astype(BF16)
    lw2 = _pad_rows(rw_w2[0], LANES).astype(BF16)
    la1 = _pad_cols(rw_a1[0], LANES).astype(BF16)
    la2 = _pad_rows(rw_a2[0], LANES).astype(BF16)
    lg1 = rw_g1[0].astype(BF16)
    lg2 = rw_g2[0].astype(BF16)
    wo = rw_wo[0].astype(BF16)
    vec = lambda v: v.reshape(1, D)
    new_wkv, new_shift, x1 = [], [], []
    for p in range(2):
        nb, tt = cfgs[p]
        sh, sc, gt = mod(0, 0, 0, p), mod(0, 0, 1, p), mod(0, 0, 2, p)
        rkv = _rkv(xs[p], sc, sh, shift0[p], mu_rkv, w_rkv, nb, tt)
        lw, a, g, last = _lora(xs[p], sc, sh, shift0[p], mu_lora, lw1, lw2, vec(rw_w0[0]), la1, la2, vec(rw_a0[0]),
                               lg1, lg2, nb, tt)
        z, s_fin = _wkv(rkv, lw, a, g, vec(rw_kk[0]), vec(rw_ka[0]), vec(rw_rk[0]), vec(rw_lnx_g[0]),
                        vec(rw_lnx_b[0]), wkv0[p], chunk_len[p])
        x1.append(_outproj_ln(z, xs[p], gt, ln_g[0, 0].reshape(1, 1, D), ln_b[0, 0].reshape(1, 1, D), wo, nb, tt))
        new_wkv.append(s_fin[None])
        new_shift.append(last.reshape(1, -1, D))

    def moe(layer, xin):
        w_router = _pad_cols(jnp.concatenate([moe_wg[layer], moe_wr[layer]], axis=1), LANES)
        b_router = _pad_cols(jnp.concatenate([moe_bg[layer], moe_br[layer]])[None, :], LANES)
        mods = [(mod(layer, 1, 0, p), mod(layer, 1, 1, p)) for p in range(2)]
        ys = _moe_layer(xin, mods, w_router, b_router, moe_w1[layer], moe_w3[layer], moe_w2[layer], cfgs)
        out = []
        for p in range(2):
            nb, tt = cfgs[p]
            out.append(_resid_ln(xin[p], ys[p], mod(layer, 1, 2, p), ln_g[layer, 1].reshape(1, 1, D),
                                 ln_b[layer, 1].reshape(1, 1, D), nb, tt))
        return out

    x2 = moe(0, x1)

    w_kv_b = w_kv.astype(BF16)
    kv = []
    for p in range(2):
        nb, tt = cfgs[p]
        kv_sh = kvm[rows[p], :D][:, None, :]
        kv_sc = kvm[rows[p], D:][:, None, :]
        kv.append(_modproj(x2[p], kv_sc, kv_sh, w_kv_b, nb, tt, F32))
    k_sh = [t[..., :KVW] for t in kv]
    v_sh = [t[..., KVW:] for t in kv]

    wq = at_wq[0].astype(BF16)
    wo1 = at_wo[0].astype(BF16)
    sinks = at_sinks[0].astype(F32)
    x3 = []
    for p in range(2):
        nb, tt = cfgs[p]
        sh, sc, gt = mod(1, 0, 0, p), mod(1, 0, 1, p), mod(1, 0, 2, p)
        q = _modproj(x2[p], sc, sh, wq, nb, tt, F32)
        if p == 0:
            o = _attn_prompt(sinks, q, _dup_heads(k_sh[p]), _dup_heads(v_sh[p]))
        else:
            k_all = jnp.concatenate([cache_k.reshape(BS, WINDOW, KVW), k_sh[p]], axis=1)
            v_all = jnp.concatenate([cache_v.reshape(BS, WINDOW, KVW), v_sh[p]], axis=1)
            o = _attn_sample(sinks, q, _dup_heads(k_all), _dup_heads(v_all))
        x3.append(_outproj_ln(o, x2[p], gt, ln_g[1, 0].reshape(1, 1, D), ln_b[1, 0].reshape(1, 1, D), wo1, nb, tt))

    x4 = moe(1, x3)

    k_p = k_sh[0][:, -WINDOW:].reshape(B, WINDOW, N_KV_HEADS, HEAD)
    v_p = v_sh[0][:, -WINDOW:].reshape(B, WINDOW, N_KV_HEADS, HEAD)
    k_s = k_sh[1].reshape(BS, TS, N_KV_HEADS, HEAD)
    v_s = v_sh[1].reshape(BS, TS, N_KV_HEADS, HEAD)
    return (x4[0], x4[1], new_wkv[0], new_shift[0], k_p, v_p, new_wkv[1], new_shift[1], k_s, v_s)
```

```python
import functools

import jax
import jax.numpy as jnp
from jax import lax
from jax.experimental import pallas as pl
from jax.experimental.pallas import tpu as pltpu

F32 = jnp.float32
BF16 = jnp.bfloat16

HEAD = 64
N_KV_HEADS = 4
WINDOW = 128
CHUNK = 64
N_GROUPS = 4
EXPERTS_PER_GROUP = 8
N_EXPERTS = N_GROUPS * EXPERTS_PER_GROUP
TOP_K = 2
DEPTH = 2
ALPHA = (2.0 * DEPTH) ** 0.25
LN_EPS = 1e-5
GN_EPS = 64e-5
NEG_INF = -1e30

LANES = 128
PAIR = LANES // HEAD

PROMPT_ROWS = 512
MOE_ROWS = 256
MODS_TN = 1024
WKV_PASSES = 1
WKV_UNROLL = 8
ATTN_HEADS_PER_STAGE = 16
VMEM_LIMIT = 56 * 1024 * 1024

NN = (((1,), (0,)), ((), ()))
NT = (((1,), (1,)), ((), ()))
TN = (((0,), (0,)), ((), ()))


def _mm(a, b, dims=NN, passes=1):
    if passes == 6:
        return lax.dot_general(a, b, dims, precision=lax.Precision.HIGHEST, preferred_element_type=F32)
    ah = a.astype(BF16)
    bh = b.astype(BF16)
    out = lax.dot_general(ah, bh, dims, preferred_element_type=F32)
    if passes == 3:
        al = (a - ah.astype(F32)).astype(BF16)
        bl = (b - bh.astype(F32)).astype(BF16)
        out = out + lax.dot_general(ah, bl, dims, preferred_element_type=F32)
        out = out + lax.dot_general(al, bh, dims, preferred_element_type=F32)
    return out


def _bf16_terms(x, terms):
    out = []
    for _ in range(terms):
        t = x.astype(BF16)
        out.append(t)
        x = x - t.astype(F32)
    return out


def _mm_mask(a, b, dims=NN, split="lhs", terms=2):
    if split == "lhs":
        bb = b.astype(BF16)
        parts = [lax.dot_general(t, bb, dims, preferred_element_type=F32) for t in _bf16_terms(a, terms)]
    else:
        aa = a.astype(BF16)
        parts = [lax.dot_general(aa, t, dims, preferred_element_type=F32) for t in _bf16_terms(b, terms)]
    return functools.reduce(lambda p, q: p + q, parts)


def _sigmoid(x):
    return 1.0 / (1.0 + jnp.exp(-x))


def _layer_norm(v, g, b):
    mu = jnp.mean(v, axis=-1, keepdims=True)
    d = v - mu
    var = jnp.mean(d * d, axis=-1, keepdims=True)
    return d * lax.rsqrt(var + LN_EPS) * g + b


def _params(*sem):
    return pltpu.CompilerParams(dimension_semantics=sem, vmem_limit_bytes=VMEM_LIMIT)


def _mods_kernel(c_ref, w_ref, b_ref, o_ref):
    c = c_ref[...]
    a = (c * _sigmoid(c)).astype(BF16)
    o_ref[0] = jnp.dot(a, w_ref[0].astype(BF16), preferred_element_type=F32) + b_ref[0]


def _mods(c, w, b):
    g, d, n = w.shape
    m = c.shape[0]
    tn = min(MODS_TN, n)
    assert n % tn == 0
    return pl.pallas_call(
        _mods_kernel,
        out_shape=jax.ShapeDtypeStruct((g, m, n), F32),
        grid=(g, n // tn),
        in_specs=[
            pl.BlockSpec((m, d), lambda i, j: (0, 0)),
            pl.BlockSpec((1, d, tn), lambda i, j: (i, 0, j)),
            pl.BlockSpec((1, 1, tn), lambda i, j: (i, 0, j)),
        ],
        out_specs=pl.BlockSpec((1, m, tn), lambda i, j: (i, 0, j)),
        compiler_params=_params("arbitrary", "arbitrary"),
        name="mods",
    )(c, w, b)


def _modulate(x_ref, sc_ref, sh_ref):
    nb, tt, d = x_ref.shape
    h = x_ref[...] * (1.0 + sc_ref[...]) + sh_ref[...]
    return h.reshape(nb * tt, d)


def _token_shift(h, prev_scr, prev0_ref, tt, t_axis):
    rows, d = h.shape
    nb = rows // tt

    @pl.when(pl.program_id(t_axis) == 0)
    def _():
        prev_scr[...] = prev0_ref[...]

    prev = jnp.broadcast_to(prev_scr[...], (nb, tt, d)).reshape(rows, d)
    rolled = pltpu.roll(h, 1, 0)
    row = lax.broadcasted_iota(jnp.int32, (rows, 1), 0)
    first = (row & (tt - 1)) == 0
    hp = jnp.where(first, prev, rolled)
    prev_scr[...] = h.reshape(nb, tt, d)[:, tt - 1:tt, :]
    return hp


def _rkv_kernel(x_ref, sc_ref, sh_ref, prev0_ref, mu_ref, w_ref, o_ref, prev_scr):
    nb, tt, d = x_ref.shape
    h = _modulate(x_ref, sc_ref, sh_ref)
    hp = _token_shift(h, prev_scr, prev0_ref, tt, 2)
    xin = h + (hp - h) * mu_ref[0]
    out = jnp.dot(xin.astype(BF16), w_ref[0], preferred_element_type=F32)
    o_ref[0] = out.reshape(nb, tt, out.shape[-1])


def _rkv(x, sc, sh, prev0, mu3, w3, nb, tt):
    NB, TT, D = x.shape
    seq = lambda j, b, t: (b, t, 0)
    per = lambda j, b, t: (b, 0, 0)
    return pl.pallas_call(
        _rkv_kernel,
        out_shape=jax.ShapeDtypeStruct((3, NB, TT, D), F32),
        grid=(3, NB // nb, TT // tt),
        in_specs=[
            pl.BlockSpec((nb, tt, D), seq),
            pl.BlockSpec((nb, 1, D), per),
            pl.BlockSpec((nb, 1, D), per),
            pl.BlockSpec((nb, 1, D), per),
            pl.BlockSpec((1, 1, D), lambda j, b, t: (j, 0, 0)),
            pl.BlockSpec((1, D, D), lambda j, b, t: (j, 0, 0)),
        ],
        out_specs=pl.BlockSpec((1, nb, tt, D), lambda j, b, t: (j, b, t, 0)),
        scratch_shapes=[pltpu.VMEM((nb, 1, D), F32)],
        compiler_params=_params("arbitrary", "arbitrary", "arbitrary"),
        name="tmix_rkv",
    )(x, sc, sh, prev0, mu3, w3)


def _lora_kernel(x_ref, sc_ref, sh_ref, prev0_ref, mu_ref, w1_ref, w2_ref, w0_ref, a1_ref, a2_ref, a0_ref,
                 g1_ref, g2_ref, lw_ref, a_ref, g_ref, last_ref, prev_scr):
    nb, tt, d = x_ref.shape
    h = _modulate(x_ref, sc_ref, sh_ref)
    hp = _token_shift(h, prev_scr, prev0_ref, tt, 1)
    dx = hp - h
    last_ref[...] = h.reshape(nb, tt, d)[:, tt - 1:tt, :]

    xw = (h + dx * mu_ref[0:1, :]).astype(BF16)
    zw = jnp.tanh(jnp.dot(xw, w1_ref[...], preferred_element_type=F32))
    zw = jnp.dot(zw.astype(BF16), w2_ref[...], preferred_element_type=F32) + w0_ref[...]
    sp = jnp.maximum(-zw, 0.0) + jnp.log1p(jnp.exp(-jnp.abs(zw)))
    lw_ref[...] = (-jnp.exp(-sp - 0.5)).reshape(nb, tt, d)

    xa = (h + dx * mu_ref[1:2, :]).astype(BF16)
    za = jnp.dot(xa, a1_ref[...], preferred_element_type=F32)
    za = jnp.dot(za.astype(BF16), a2_ref[...], preferred_element_type=F32) + a0_ref[...]
    a_ref[...] = _sigmoid(za).reshape(nb, tt, d)

    xg = (h + dx * mu_ref[2:3, :]).astype(BF16)
    zg = _sigmoid(jnp.dot(xg, g1_ref[...], preferred_element_type=F32))
    g_ref[...] = jnp.dot(zg.astype(BF16), g2_ref[...], preferred_element_type=F32).reshape(nb, tt, d)


def _lora(x, sc, sh, prev0, mu3, w1, w2, w0, a1, a2, a0, g1, g2, nb, tt):
    NB, TT, D = x.shape
    seq = lambda b, t: (b, t, 0)
    per = lambda b, t: (b, 0, 0)
    full = lambda a: pl.BlockSpec(a.shape, lambda b, t: (0,) * a.ndim)
    big = jax.ShapeDtypeStruct((NB, TT, D), F32)
    return pl.pallas_call(
        _lora_kernel,
        out_shape=(big, big, big, jax.ShapeDtypeStruct((NB, 1, D), F32)),
        grid=(NB // nb, TT // tt),
        in_specs=[
            pl.BlockSpec((nb, tt, D), seq),
            pl.BlockSpec((nb, 1, D), per),
            pl.BlockSpec((nb, 1, D), per),
            pl.BlockSpec((nb, 1, D), per),
            full(mu3), full(w1), full(w2), full(w0), full(a1), full(a2), full(a0), full(g1), full(g2),
        ],
        out_specs=(pl.BlockSpec((nb, tt, D), seq),) * 3 + (pl.BlockSpec((nb, 1, D), per),),
        scratch_shapes=[pltpu.VMEM((nb, 1, D), F32)],
        compiler_params=_params("arbitrary", "arbitrary"),
        name="tmix_lora",
    )(x, sc, sh, prev0, mu3, w1, w2, w0, a1, a2, a0, g1, g2)


def _wkv_kernel(r_ref, k_ref, v_ref, lw_ref, a_ref, g_ref, kkc_ref, kac_ref, rk_ref, lng_ref, lnb_ref, s0_ref,
                z_ref, sfin_ref, s_scr, *, passes, unroll):
    L = r_ref.shape[2]
    n_heads = s_scr.shape[0]
    unroll = min(unroll, n_heads // PAIR)
    c = pl.program_id(1)

    @pl.when(c == 0)
    def _():
        s_scr[...] = jnp.zeros(s_scr.shape, F32)
        for h in range(n_heads):
            o = (h % PAIR) * HEAD
            s_scr[h, o:o + HEAD, o:o + HEAD] = s0_ref[0, h]

    lane = lax.broadcasted_iota(jnp.int32, (1, LANES), 1)
    ri = lax.broadcasted_iota(jnp.int32, (L, L), 0)
    ci = lax.broadcasted_iota(jnp.int32, (L, L), 1)
    strict = ri > ci
    incl = ri >= ci
    tril = jnp.where(incl, 1.0, 0.0).astype(F32)
    bi = lax.broadcasted_iota(jnp.int32, (LANES, LANES), 0) // HEAD
    bj = lax.broadcasted_iota(jnp.int32, (LANES, LANES), 1) // HEAD
    head_ones = jnp.where(bi == bj, 1.0, 0.0).astype(F32)

    def group_body(gi, carry):
        pairs = [gi * unroll + u for u in range(unroll)]
        sls = [pl.ds(pl.multiple_of(p * LANES, LANES), LANES) for p in pairs]
        heads = [(i, hh) for i in range(unroll) for hh in range(PAIR)]
        masks = [lane < HEAD, lane >= HEAD]
        s0 = [s_scr[PAIR * pairs[i] + hh] for i, hh in heads]

        rp = [r_ref[0, 0, :, sl] for sl in sls]
        kp = [k_ref[0, 0, :, sl] for sl in sls]
        vp = [v_ref[0, 0, :, sl] for sl in sls]
        lwp = [lw_ref[0, :, sl] for sl in sls]
        ap = [a_ref[0, :, sl] for sl in sls]
        cum = [_mm_mask(tril, t, split="rhs", terms=3) for t in lwp]
        kk0 = [k * kkc_ref[:, sl] for k, sl in zip(kp, sls)]
        n2 = [_mm_mask(t * t, head_ones) for t in kk0]
        kkp = [t * lax.rsqrt(jnp.maximum(n, 1e-24)) for t, n in zip(kk0, n2)]
        khp = [k * (1.0 + (a - 1.0) * kac_ref[:, sl]) for k, a, sl in zip(kp, ap, sls)]
        c_incl = [jnp.exp(t) for t in cum]
        c_inv = [jnp.exp(-t) for t in cum]
        kt = [kk * jnp.exp(t - lw) for kk, t, lw in zip(kkp, cum, lwp)]
        rt = [r * ci_ for r, ci_ in zip(rp, c_incl)]
        kti = [kh * cv for kh, cv in zip(khp, c_inv)]
        bti = [kk * a * cv for kk, a, cv in zip(kkp, ap, c_inv)]
        lhs = [jnp.concatenate([a_, b_], axis=0) for a_, b_ in zip(kt, rt)]

        lhs_m = [jnp.where(masks[hh], lhs[i], 0.0) for i, hh in heads]
        gk = [_mm(lhs_m[j], kti[i], NT, passes) for j, (i, hh) in enumerate(heads)]
        gb = [_mm(lhs_m[j], bti[i], NT, passes) for j, (i, hh) in enumerate(heads)]
        p0 = [_mm(lhs[i], s0[j], NT, passes) for j, (i, hh) in enumerate(heads)]
        a_kk = [jnp.where(strict, t[:L], 0.0) for t in gk]
        a_rk = [jnp.where(incl, t[L:], 0.0) for t in gk]
        a_rb = [jnp.where(incl, t[L:], 0.0) for t in gb]
        pw = [jnp.where(strict, -t[:L], 0.0) for t in gb]
        x = [p0[j][:L] + _mm(a_kk[j], vp[i], NN, passes) for j, (i, hh) in enumerate(heads)]
        u = [x_ + _mm(p_, x_, NN, passes) for p_, x_ in zip(pw, x)]
        n = 1
        while n * 2 < L:
            pw = [_mm(p_, p_, NN, passes) for p_ in pw]
            u = [u_ + _mm(p_, u_, NN, passes) for p_, u_ in zip(pw, u)]
            n *= 2
        y = [p0[j][L:] + _mm(a_rk[j], vp[i], NN, passes) - _mm(a_rb[j], u[j], NN, passes)
             for j, (i, hh) in enumerate(heads)]
        vu = [jnp.where(masks[hh], jnp.concatenate([vp[i], u[j]], axis=0), 0.0) for j, (i, hh) in enumerate(heads)]
        kb = [jnp.where(masks[hh], jnp.concatenate([kti[i], -bti[i]], axis=0), 0.0) for i, hh in heads]
        ds = [_mm(vu_, kb_, TN, passes) for vu_, kb_ in zip(vu, kb)]
        s_new = [(s0[j] + ds[j]) * c_incl[i][L - 1:L, :] for j, (i, hh) in enumerate(heads)]

        yp = [jnp.where(masks[0], y[PAIR * i], y[PAIR * i + 1]) for i in range(unroll)]
        mean = [_mm_mask(t, head_ones) * (1.0 / HEAD) for t in yp]
        dlt = [t - m_ for t, m_ in zip(yp, mean)]
        var = [_mm_mask(t * t, head_ones) * (1.0 / HEAD) for t in dlt]
        bonus = [_mm_mask(r * kh * rk_ref[:, sl], head_ones) * v
                 for r, kh, v, sl in zip(rp, khp, vp, sls)]
        for i, sl in enumerate(sls):
            yn = dlt[i] * lax.rsqrt(var[i] + GN_EPS) * lng_ref[:, sl] + lnb_ref[:, sl]
            z_ref[0, :, sl] = ((yn + bonus[i]) * g_ref[0, :, sl]).astype(z_ref.dtype)
        for j, (i, hh) in enumerate(heads):
            s_scr[PAIR * pairs[i] + hh] = s_new[j]
        return carry

    assert (n_heads // PAIR) % unroll == 0
    lax.fori_loop(0, n_heads // PAIR // unroll, group_body, 0)

    @pl.when(c == pl.num_programs(1) - 1)
    def _():
        for h in range(n_heads):
            o = (h % PAIR) * HEAD
            sfin_ref[0, h] = s_scr[h, o:o + HEAD, o:o + HEAD]


def _wkv(rkv, lw, a, g, kkc, kac, rk, lng, lnb, s0, L, passes=WKV_PASSES, unroll=WKV_UNROLL):
    _, NB, TT, D = rkv.shape
    H = D // HEAD
    seq = lambda b, t: (b, t, 0)
    vec = pl.BlockSpec((1, D), lambda b, t: (0, 0))
    st = pl.BlockSpec((1, H, HEAD, HEAD), lambda b, t: (b, 0, 0, 0))
    return pl.pallas_call(
        functools.partial(_wkv_kernel, passes=passes, unroll=unroll),
        out_shape=(jax.ShapeDtypeStruct((NB, TT, D), BF16), jax.ShapeDtypeStruct((NB, H, HEAD, HEAD), F32)),
        grid=(NB, TT // L),
        in_specs=[
            pl.BlockSpec((1, 1, L, D), lambda b, t: (0, b, t, 0)),
            pl.BlockSpec((1, 1, L, D), lambda b, t: (1, b, t, 0)),
            pl.BlockSpec((1, 1, L, D), lambda b, t: (2, b, t, 0)),
            pl.BlockSpec((1, L, D), seq),
            pl.BlockSpec((1, L, D), seq),
            pl.BlockSpec((1, L, D), seq),
            vec, vec, vec, vec, vec, st,
        ],
        out_specs=(pl.BlockSpec((1, L, D), seq), st),
        scratch_shapes=[pltpu.VMEM((H, LANES, LANES), F32)],
        compiler_params=_params("arbitrary", "arbitrary"),
        name="wkv",
    )(rkv, rkv, rkv, lw, a, g, kkc, kac, rk, lng, lnb, s0)


def _outproj_ln_kernel(a_ref, x_ref, gate_ref, lng_ref, lnb_ref, w_ref, o_ref):
    nb, tt, d = x_ref.shape
    a = a_ref[...].reshape(nb * tt, a_ref.shape[-1])
    y = jnp.dot(a, w_ref[...], preferred_element_type=F32).reshape(nb, tt, d)
    v = ALPHA * x_ref[...] + (1.0 + gate_ref[...]) * y
    o_ref[...] = _layer_norm(v, lng_ref[...], lnb_ref[...])


def _outproj_ln(a, x, gate, lng, lnb, w, nb, tt):
    NB, TT, D = x.shape
    seq = lambda b, t: (b, t, 0)
    per = lambda b, t: (b, 0, 0)
    vec = pl.BlockSpec((1, 1, D), lambda b, t: (0, 0, 0))
    return pl.pallas_call(
        _outproj_ln_kernel,
        out_shape=jax.ShapeDtypeStruct((NB, TT, D), F32),
        grid=(NB // nb, TT // tt),
        in_specs=[
            pl.BlockSpec((nb, tt, a.shape[-1]), seq),
            pl.BlockSpec((nb, tt, D), seq),
            pl.BlockSpec((nb, 1, D), per),
            vec, vec,
            pl.BlockSpec(w.shape, lambda b, t: (0, 0)),
        ],
        out_specs=pl.BlockSpec((nb, tt, D), seq),
        compiler_params=_params("arbitrary", "arbitrary"),
        name="outproj_ln",
    )(a, x, gate, lng, lnb, w)


def _modproj_kernel(x_ref, sc_ref, sh_ref, w_ref, o_ref):
    nb, tt, d = x_ref.shape
    h = _modulate(x_ref, sc_ref, sh_ref)
    out = jnp.dot(h.astype(BF16), w_ref[...], preferred_element_type=F32)
    o_ref[...] = out.reshape(nb, tt, out.shape[-1]).astype(o_ref.dtype)


def _modproj(x, sc, sh, w, nb, tt, out_dtype):
    NB, TT, D = x.shape
    N = w.shape[1]
    seq = lambda b, t: (b, t, 0)
    per = lambda b, t: (b, 0, 0)
    return pl.pallas_call(
        _modproj_kernel,
        out_shape=jax.ShapeDtypeStruct((NB, TT, N), out_dtype),
        grid=(NB // nb, TT // tt),
        in_specs=[
            pl.BlockSpec((nb, tt, D), seq),
            pl.BlockSpec((nb, 1, D), per),
            pl.BlockSpec((nb, 1, D), per),
            pl.BlockSpec(w.shape, lambda b, t: (0, 0)),
        ],
        out_specs=pl.BlockSpec((nb, tt, N), seq),
        compiler_params=_params("arbitrary", "arbitrary"),
        name="modproj",
    )(x, sc, sh, w)


ROUTE_E, ROUTE_RANK, ROUTE_GATE = 0, 2, 4
BIG_NEG = -3.0e38


def _route_kernel(x_ref, sc_ref, sh_ref, w_ref, b_ref, cnt0_ref, route_ref, cnt_ref, cnt_scr):
    first = (pl.program_id(0) == 0) & (pl.program_id(1) == 0)

    @pl.when(first)
    def _():
        cnt_scr[...] = cnt0_ref[...]

    h = _modulate(x_ref, sc_ref, sh_ref)
    rows = h.shape[0]
    lg = _mm(h, w_ref[...], NN, passes=3) + b_ref[...]
    lane = lax.broadcasted_iota(jnp.int32, (1, LANES), 1)

    def first_argmax(v, vmax):
        return jnp.min(jnp.where(v == vmax, lane, LANES), axis=-1, keepdims=True)

    is_g = lane < N_GROUPS
    gl = jnp.where(is_g, lg, BIG_NEG)
    gmax = jnp.max(gl, axis=-1, keepdims=True)
    g_sel = first_argmax(gl, gmax)
    g_prob = 1.0 / jnp.sum(jnp.where(is_g, jnp.exp(lg - gmax), 0.0), axis=-1, keepdims=True)
    lo = N_GROUPS + EXPERTS_PER_GROUP * g_sel
    el = jnp.where((lane >= lo) & (lane < lo + EXPERTS_PER_GROUP), lg, BIG_NEG)
    v1 = jnp.max(el, axis=-1, keepdims=True)
    i1 = first_argmax(el, v1)
    el2 = jnp.where(lane == i1, BIG_NEG, el)
    v2 = jnp.max(el2, axis=-1, keepdims=True)
    i2 = first_argmax(el2, v2)
    e1 = i1 - N_GROUPS
    e2 = i2 - N_GROUPS
    t = jnp.exp(v2 - v1)
    p1 = 1.0 / (1.0 + t)
    gate1 = p1 * g_prob
    gate2 = t * p1 * g_prob

    oh1 = jnp.where(lane == e1, 1.0, 0.0)
    oh2 = jnp.where(lane == e2, 1.0, 0.0)
    ri = lax.broadcasted_iota(jnp.int32, (rows, rows), 0)
    ci = lax.broadcasted_iota(jnp.int32, (rows, rows), 1)
    before = jnp.where(ri > ci, 1.0, 0.0).astype(BF16)
    c1 = jnp.dot(before, oh1.astype(BF16), preferred_element_type=F32)
    c2 = jnp.dot(before, oh2.astype(BF16), preferred_element_type=F32)
    tot1 = jnp.sum(oh1, axis=0, keepdims=True)
    tot2 = jnp.sum(oh2, axis=0, keepdims=True)
    base = cnt_scr[...]
    rank1 = jnp.sum(oh1 * (base + c1), axis=-1, keepdims=True)
    rank2 = jnp.sum(oh2 * (base + tot1 + c2), axis=-1, keepdims=True)
    cnt_new = base + tot1 + tot2
    cnt_scr[...] = cnt_new
    cnt_ref[...] = cnt_new

    out = jnp.where(lane == ROUTE_E, e1.astype(F32), 0.0)
    out = jnp.where(lane == ROUTE_E + 1, e2.astype(F32), out)
    out = jnp.where(lane == ROUTE_RANK, rank1, out)
    out = jnp.where(lane == ROUTE_RANK + 1, rank2, out)
    out = jnp.where(lane == ROUTE_GATE, gate1, out)
    out = jnp.where(lane == ROUTE_GATE + 1, gate2, out)
    route_ref[...] = out


def _route(x, sc, sh, w, b, cnt0, nb, tt):
    NB, TT, D = x.shape
    steps_t = TT // tt
    seq = lambda b_, t: (b_, t, 0)
    per = lambda b_, t: (b_, 0, 0)
    one = lambda b_, t: (0, 0)
    return pl.pallas_call(
        _route_kernel,
        out_shape=(jax.ShapeDtypeStruct((NB * TT, LANES), F32), jax.ShapeDtypeStruct((1, LANES), F32)),
        grid=(NB // nb, steps_t),
        in_specs=[
            pl.BlockSpec((nb, tt, D), seq),
            pl.BlockSpec((nb, 1, D), per),
            pl.BlockSpec((nb, 1, D), per),
            pl.BlockSpec(w.shape, one),
            pl.BlockSpec(b.shape, one),
            pl.BlockSpec((1, LANES), one),
        ],
        out_specs=(pl.BlockSpec((nb * tt, LANES), lambda b_, t: (b_ * steps_t + t, 0)),
                   pl.BlockSpec((1, LANES), one)),
        scratch_shapes=[pltpu.VMEM((1, LANES), F32)],
        compiler_params=_params("arbitrary", "arbitrary"),
        name="route",
    )(x, sc, sh, w, b, cnt0)


def _row_copy(src, dst, sem):
    return pltpu.make_async_copy(src, dst, sem)


def _dispatch_kernel(dest_ref, x_ref, sc_ref, sh_ref, xb_in_ref, xb_ref, buf, sems):
    del xb_in_ref
    steps_t = pl.num_programs(1)
    i = pl.program_id(0) * steps_t + pl.program_id(1)
    n_steps = pl.num_programs(0) * steps_t
    slot = i % 2
    rows, half = buf.shape[1:]

    def wait_slot(s):
        def body(r, carry):
            _row_copy(buf.at[s, pl.ds(0, 1)], xb_ref.at[pl.ds(0, 1)], sems.at[s]).wait()
            return carry
        lax.fori_loop(0, rows * TOP_K, body, 0)

    @pl.when(i >= 2)
    def _():
        wait_slot(slot)

    h = _modulate(x_ref, sc_ref, sh_ref)
    bits = pltpu.bitcast(h.astype(BF16).astype(F32), jnp.uint32)
    buf[slot] = bits[:, :half] | (bits[:, half:] >> 16)

    def issue(r, carry):
        for j in range(TOP_K):
            d = dest_ref[TOP_K * r + j]
            _row_copy(buf.at[slot, pl.ds(r, 1)], xb_ref.at[pl.ds(d, 1)], sems.at[slot]).start()
        return carry
    lax.fori_loop(0, rows, issue, 0)

    @pl.when(i == n_steps - 1)
    def _():
        wait_slot(slot)

    @pl.when((i == n_steps - 1) & (i >= 1))
    def _():
        wait_slot(1 - slot)


def _moe_dispatch(dest_flat, x, sc, sh, xb, nb, tt):
    NB, TT, D = x.shape
    steps_t = TT // tt
    rows = nb * tt
    seq = lambda b_, t: (b_, t, 0)
    per = lambda b_, t: (b_, 0, 0)
    return pl.pallas_call(
        _dispatch_kernel,
        out_shape=jax.ShapeDtypeStruct(xb.shape, xb.dtype),
        grid=(NB // nb, steps_t),
        in_specs=[
            pl.BlockSpec((rows * TOP_K,), lambda b_, t: (b_ * steps_t + t,), memory_space=pltpu.SMEM),
            pl.BlockSpec((nb, tt, D), seq),
            pl.BlockSpec((nb, 1, D), per),
            pl.BlockSpec((nb, 1, D), per),
            pl.BlockSpec(memory_space=pl.ANY),
        ],
        out_specs=pl.BlockSpec(memory_space=pl.ANY),
        scratch_shapes=[pltpu.VMEM((2, rows, D // 2), jnp.uint32), pltpu.SemaphoreType.DMA((2,))],
        input_output_aliases={4: 0},
        compiler_params=_params("arbitrary", "arbitrary"),
        name="moe_dispatch",
    )(dest_flat, x, sc, sh, xb)


def _moe_kernel(be_ref, used_ref, x_ref, w1_ref, w3_ref, w2_ref, o_ref, w1s, w3s, w2s):
    i = pl.program_id(0)
    e = be_ref[i]
    e_prev = be_ref[jnp.maximum(i - 1, 0)]

    @pl.when((i == 0) | (e != e_prev))
    def _():
        w1s[...] = w1_ref[0].astype(BF16)
        w3s[...] = w3_ref[0].astype(BF16)
        w2s[...] = w2_ref[0].astype(BF16)

    @pl.when(i < used_ref[0])
    def _():
        p = x_ref[...]
        hi = pltpu.bitcast(p & jnp.uint32(0xFFFF0000), F32)
        lo = pltpu.bitcast(p << 16, F32)
        x = jnp.concatenate([hi, lo], axis=1).astype(BF16)
        h1 = jnp.dot(x, w1s[...], preferred_element_type=F32)
        h3 = jnp.dot(x, w3s[...], preferred_element_type=F32)
        act = (h1 * _sigmoid(h1) * h3).astype(BF16)
        o_ref[...] = jnp.dot(act, w2s[...], preferred_element_type=F32)

    @pl.when(i >= used_ref[0])
    def _():
        o_ref[...] = jnp.zeros(o_ref.shape, F32)


def _moe_ffn(blk_expert, n_used, xb, w1, w3, w2):
    rows, half = xb.shape
    E, D, DE = w1.shape
    n_blocks = rows // MOE_ROWS
    return pl.pallas_call(
        _moe_kernel,
        out_shape=jax.ShapeDtypeStruct((rows, D), F32),
        grid_spec=pltpu.PrefetchScalarGridSpec(
            num_scalar_prefetch=2,
            grid=(n_blocks,),
            in_specs=[
                pl.BlockSpec((MOE_ROWS, half), lambda i, be, nu: (i, 0)),
                pl.BlockSpec((1, D, DE), lambda i, be, nu: (be[i], 0, 0)),
                pl.BlockSpec((1, D, DE), lambda i, be, nu: (be[i], 0, 0)),
                pl.BlockSpec((1, DE, D), lambda i, be, nu: (be[i], 0, 0)),
            ],
            out_specs=pl.BlockSpec((MOE_ROWS, D), lambda i, be, nu: (i, 0)),
            scratch_shapes=[pltpu.VMEM((D, DE), BF16), pltpu.VMEM((D, DE), BF16), pltpu.VMEM((DE, D), BF16)],
        ),
        compiler_params=_params("arbitrary"),
        name="moe_ffn",
    )(blk_expert, n_used, xb, w1, w3, w2)


def _combine_ln_kernel(dest_ref, dest_next_ref, x_ref, route_ref, gate_ref, lng_ref, lnb_ref, yb_ref, o_ref,
                       ybuf, sems):
    steps_t = pl.num_programs(1)
    i = pl.program_id(0) * steps_t + pl.program_id(1)
    n_steps = pl.num_programs(0) * steps_t
    slot = i % 2
    nb, tt, d = x_ref.shape
    rows = nb * tt

    def fetch(d_ref, s):
        def body(r, carry):
            for j in range(TOP_K):
                row = d_ref[TOP_K * r + j]
                _row_copy(yb_ref.at[pl.ds(row, 1)], ybuf.at[s, j, pl.ds(r, 1)], sems.at[s]).start()
            return carry
        lax.fori_loop(0, rows, body, 0)

    @pl.when(i == 0)
    def _():
        fetch(dest_ref, slot)

    def wait_body(r, carry):
        _row_copy(yb_ref.at[pl.ds(0, 1)], ybuf.at[slot, 0, pl.ds(0, 1)], sems.at[slot]).wait()
        return carry
    lax.fori_loop(0, rows * TOP_K, wait_body, 0)

    @pl.when(i + 1 < n_steps)
    def _():
        fetch(dest_next_ref, 1 - slot)

    rt = route_ref[...]
    y = rt[:, ROUTE_GATE:ROUTE_GATE + 1] * ybuf[slot, 0] + rt[:, ROUTE_GATE + 1:ROUTE_GATE + 2] * ybuf[slot, 1]
    v = ALPHA * x_ref[...] + (1.0 + gate_ref[...]) * y.reshape(nb, tt, d)
    o_ref[...] = _layer_norm(v, lng_ref[...], lnb_ref[...])


def _combine_ln(dest_flat, x, route, gate, lng, lnb, yb, nb, tt, row0):
    NB, TT, D = x.shape
    steps_t = TT // tt
    n_steps = (NB // nb) * steps_t
    rows = nb * tt
    blk0 = row0 // rows
    seq = lambda b_, t: (b_, t, 0)
    per = lambda b_, t: (b_, 0, 0)
    vec = pl.BlockSpec((1, 1, D), lambda b_, t: (0, 0, 0))
    cur = lambda b_, t: (blk0 + b_ * steps_t + t,)
    nxt = lambda b_, t: (blk0 + jnp.minimum(b_ * steps_t + t + 1, n_steps - 1),)
    return pl.pallas_call(
        _combine_ln_kernel,
        out_shape=jax.ShapeDtypeStruct((NB, TT, D), F32),
        grid=(NB // nb, steps_t),
        in_specs=[
            pl.BlockSpec((rows * TOP_K,), cur, memory_space=pltpu.SMEM),
            pl.BlockSpec((rows * TOP_K,), nxt, memory_space=pltpu.SMEM),
            pl.BlockSpec((nb, tt, D), seq),
            pl.BlockSpec((rows, LANES), lambda b_, t: (blk0 + b_ * steps_t + t, 0)),
            pl.BlockSpec((nb, 1, D), per),
            vec, vec,
            pl.BlockSpec(memory_space=pl.ANY),
        ],
        out_specs=pl.BlockSpec((nb, tt, D), seq),
        scratch_shapes=[pltpu.VMEM((2, TOP_K, rows, D), F32), pltpu.SemaphoreType.DMA((2,))],
        compiler_params=_params("arbitrary", "arbitrary"),
        name="moe_combine_ln",
    )(dest_flat, dest_flat, x, route, gate, lng, lnb, yb)


def _moe_sublayer(xs, mods, gates, lng, lnb, w_router, b_router, w1, w3, w2, cfgs):
    D = xs[0].shape[-1]
    routes = []
    cnt = jnp.zeros((1, LANES), F32)
    for x, (sh, sc), (nb, tt) in zip(xs, mods, cfgs):
        route, cnt = _route(x, sc, sh, w_router, b_router, cnt, nb, tt)
        routes.append(route)
    route = jnp.concatenate(routes, axis=0)
    n = route.shape[0]
    counts = cnt[0, :N_EXPERTS].astype(jnp.int32)
    padded = (counts + MOE_ROWS - 1) // MOE_ROWS * MOE_ROWS
    pend = jnp.cumsum(padded)
    pstart = pend - padded
    eid = route[:, ROUTE_E:ROUTE_E + TOP_K].astype(jnp.int32)
    rank = route[:, ROUTE_RANK:ROUTE_RANK + TOP_K].astype(jnp.int32)
    dest = (jnp.take(pstart, eid, axis=0) + rank).reshape(n * TOP_K)
    n_blocks = -(-n * TOP_K // MOE_ROWS) + N_EXPERTS
    blk_start = jnp.arange(n_blocks, dtype=jnp.int32) * MOE_ROWS
    blk_expert = jnp.minimum(jnp.searchsorted(pend, blk_start, side='right'), N_EXPERTS - 1).astype(jnp.int32)
    n_used = (pend[-1] // MOE_ROWS).astype(jnp.int32).reshape(1)

    xb = jnp.zeros((n_blocks * MOE_ROWS, D // 2), jnp.uint32)
    row0s = []
    row = 0
    for x, (sh, sc), (nb, tt) in zip(xs, mods, cfgs):
        rows_p = x.shape[0] * x.shape[1]
        xb = _moe_dispatch(lax.dynamic_slice_in_dim(dest, row * TOP_K, rows_p * TOP_K), x, sc, sh, xb, nb, tt)
        row0s.append(row)
        row += rows_p
    yb = _moe_ffn(blk_expert, n_used, xb, w1, w3, w2)
    return [_combine_ln(dest, x, route, gt, lng, lnb, yb, nb, tt, row0)
            for x, gt, (nb, tt), row0 in zip(xs, gates, cfgs, row0s)]


def _attn_kernel(sinks_ref, q_ref, *refs, n_band, first_key_chunk):
    k_refs = refs[:n_band]
    v_refs = refs[n_band:2 * n_band]
    o_ref = refs[2 * n_band]
    tq = q_ref.shape[1]
    kd = jnp.concatenate([r[0] for r in k_refs], axis=0) if n_band > 1 else k_refs[0][0]
    vd = jnp.concatenate([r[0] for r in v_refs], axis=0) if n_band > 1 else v_refs[0][0]
    nk = kd.shape[0]
    qi = lax.broadcasted_iota(jnp.int32, (tq, nk), 0)
    kj = lax.broadcasted_iota(jnp.int32, (tq, nk), 1)
    dist = jnp.abs((nk - tq) + qi - kj).astype(F32)
    if first_key_chunk is not None:
        k_pos = (pl.program_id(1) + first_key_chunk) * tq + kj
        valid = k_pos >= 0
    lane = lax.broadcasted_iota(jnp.int32, (1, LANES), 1)
    n_q_heads = q_ref.shape[2] // HEAD
    q_per_kv = n_q_heads // N_KV_HEADS
    scale = HEAD ** -0.5
    k_half, v_half = [], []
    for g in range(N_KV_HEADS):
        kg = kd[:, g * LANES:(g + 1) * LANES]
        vg = vd[:, g * LANES:(g + 1) * LANES]
        zero = jnp.zeros_like(kg)
        k_half.append([jnp.where(lane < HEAD, kg, zero), jnp.where(lane >= HEAD, kg, zero)])
        v_half.append([jnp.where(lane < HEAD, vg, zero), jnp.where(lane >= HEAD, vg, zero)])
    per_stage = min(ATTN_HEADS_PER_STAGE, n_q_heads)
    for h0 in range(0, n_q_heads, per_stage):
        hs = list(range(h0, h0 + per_stage))
        qp = {h // PAIR: q_ref[0, :, (h // PAIR) * LANES:(h // PAIR + 1) * LANES].astype(BF16) for h in hs}
        s = [lax.dot_general(qp[h // PAIR], k_half[h // q_per_kv][h % PAIR], NT, preferred_element_type=F32)
             for h in hs]
        s = [t * scale - (2.0 ** (-8.0 * (h + 1) / n_q_heads)) * dist for t, h in zip(s, hs)]
        if first_key_chunk is not None:
            s = [jnp.where(valid, t, NEG_INF) for t in s]
        mx = [jnp.maximum(jnp.max(t, axis=-1, keepdims=True), sinks_ref[h]) for t, h in zip(s, hs)]
        pr = [jnp.exp(t - m) for t, m in zip(s, mx)]
        den = [jnp.sum(t, axis=-1, keepdims=True) + jnp.exp(sinks_ref[h] - m) for t, m, h in zip(pr, mx, hs)]
        pr = [(t / d).astype(BF16) for t, d in zip(pr, den)]
        o = [jnp.dot(t, v_half[h // q_per_kv][h % PAIR], preferred_element_type=F32) for t, h in zip(pr, hs)]
        for j in range(0, len(hs), PAIR):
            pair = hs[j] // PAIR
            o_ref[0, :, pair * LANES:(pair + 1) * LANES] = (o[j] + o[j + 1]).astype(o_ref.dtype)


def _attn_prompt(sinks, q, kd, vd):
    B, T, D = q.shape
    nc = T // CHUNK
    w_chunks = WINDOW // CHUNK
    n_band = w_chunks + 1
    KW = kd.shape[-1]
    band = [pl.BlockSpec((1, CHUNK, KW), (lambda b, c, s, j=j: (b, jnp.maximum(c - w_chunks + j, 0), 0)))
            for j in range(n_band)]
    return pl.pallas_call(
        functools.partial(_attn_kernel, n_band=n_band, first_key_chunk=-w_chunks),
        out_shape=jax.ShapeDtypeStruct((B, T, D), BF16),
        grid_spec=pltpu.PrefetchScalarGridSpec(
            num_scalar_prefetch=1,
            grid=(B, nc),
            in_specs=[pl.BlockSpec((1, CHUNK, D), lambda b, c, s: (b, c, 0))] + band + band,
            out_specs=pl.BlockSpec((1, CHUNK, D), lambda b, c, s: (b, c, 0)),
        ),
        compiler_params=_params("arbitrary", "arbitrary"),
        name="attn_prompt",
    )(sinks, q, *([kd] * n_band), *([vd] * n_band))


def _attn_sample(sinks, q, kd, vd):
    B, T, D = q.shape
    NK, KW = kd.shape[1:]
    return pl.pallas_call(
        functools.partial(_attn_kernel, n_band=1, first_key_chunk=None),
        out_shape=jax.ShapeDtypeStruct((B, T, D), BF16),
        grid_spec=pltpu.PrefetchScalarGridSpec(
            num_scalar_prefetch=1,
            grid=(B, 1),
            in_specs=[pl.BlockSpec((1, T, D), lambda b, c, s: (b, 0, 0)),
                      pl.BlockSpec((1, NK, KW), lambda b, c, s: (b, 0, 0)),
                      pl.BlockSpec((1, NK, KW), lambda b, c, s: (b, 0, 0))],
            out_specs=pl.BlockSpec((1, T, D), lambda b, c, s: (b, 0, 0)),
        ),
        compiler_params=_params("arbitrary", "arbitrary"),
        name="attn_sample",
    )(sinks, q, kd, vd)


def _dup_heads(t):
    B, T, _ = t.shape
    t4 = t.reshape(B, T, N_KV_HEADS, 1, HEAD)
    return jnp.broadcast_to(t4, (B, T, N_KV_HEADS, PAIR, HEAD)).reshape(B, T, N_KV_HEADS * LANES).astype(BF16)


def _pad_cols(w, n):
    return jnp.pad(w, ((0, 0), (0, n - w.shape[1])))


def _pad_rows(w, n):
    return jnp.pad(w, ((0, n - w.shape[0]), (0, 0)))


def kernel(x_prompt, x_sample, c_prompt, c_sample, state_wkv, state_shift, cache_k, cache_v, mod_w, mod_b, ln_g, ln_b, rw_mu, rw_wr, rw_wk, rw_wv, rw_w0, rw_w1, rw_w2, rw_a0, rw_a1, rw_a2, rw_g1, rw_g2, rw_kk, rw_ka, rw_rk, rw_lnx_g, rw_lnx_b, rw_wo, kv_mod_w, kv_mod_b, w_kv, at_wq, at_sinks, at_wo, moe_wg, moe_bg, moe_wr, moe_br, moe_w1, moe_w3, moe_w2):
    B, T, D = x_prompt.shape
    BS, TS, _ = x_sample.shape
    H = D // HEAD
    KVW = N_KV_HEADS * HEAD
    cfgs = [(1, min(PROMPT_ROWS, T)), (BS, TS)]
    chunk_len = [CHUNK, TS]

    c_all = jnp.concatenate([c_prompt, c_sample], axis=0)
    m_all = _mods(c_all, mod_w.reshape(DEPTH * 2, D, 3 * D), mod_b.reshape(DEPTH * 2, 1, 3 * D))
    kvm = _mods(c_all, kv_mod_w[None], kv_mod_b[None, None])[0]
    rows = [slice(0, B), slice(B, B + BS)]

    def mod(layer, sub, part, path):
        return m_all[layer * 2 + sub, rows[path], part * D:(part + 1) * D][:, None, :]

    xs = [x_prompt, x_sample]
    shift0 = [jnp.zeros((B, 1, D), F32), state_shift[0][:, None, :]]
    wkv0 = [jnp.zeros((B, H, HEAD, HEAD), F32), state_wkv[0]]

    mu = rw_mu[0]
    mu_rkv = jnp.stack([mu[0], mu[2], mu[3]])[:, None, :]
    mu_lora = jnp.stack([mu[1], mu[4], mu[5]])
    w_rkv = jnp.stack([rw_wr[0], rw_wk[0], rw_wv[0]]).astype(BF16)
    lw1 = _pad_cols(rw_w1[0], LANES).astype(BF16)
    lw2 = _pad_rows(rw_w2[0], LANES).astype(BF16)
    la1 = _pad_cols(rw_a1[0], LANES).astype(BF16)
    la2 = _pad_rows(rw_a2[0], LANES).astype(BF16)
    lg1 = rw_g1[0].astype(BF16)
    lg2 = rw_g2[0].astype(BF16)
    wo = rw_wo[0].astype(BF16)
    vec = lambda v: v.reshape(1, D)
    new_wkv, new_shift, x1 = [], [], []
    for p in range(2):
        nb, tt = cfgs[p]
        sh, sc, gt = mod(0, 0, 0, p), mod(0, 0, 1, p), mod(0, 0, 2, p)
        rkv = _rkv(xs[p], sc, sh, shift0[p], mu_rkv, w_rkv, nb, tt)
        lw, a, g, last = _lora(xs[p], sc, sh, shift0[p], mu_lora, lw1, lw2, vec(rw_w0[0]), la1, la2, vec(rw_a0[0]),
                               lg1, lg2, nb, tt)
        z, s_fin = _wkv(rkv, lw, a, g, vec(rw_kk[0]), vec(rw_ka[0]), vec(rw_rk[0]), vec(rw_lnx_g[0]),
                        vec(rw_lnx_b[0]), wkv0[p], chunk_len[p])
        x1.append(_outproj_ln(z, xs[p], gt, ln_g[0, 0].reshape(1, 1, D), ln_b[0, 0].reshape(1, 1, D), wo, nb, tt))
        new_wkv.append(s_fin[None])
        new_shift.append(last.reshape(1, -1, D))

    def moe(layer, xin):
        w_router = _pad_cols(jnp.concatenate([moe_wg[layer], moe_wr[layer]], axis=1), LANES)
        b_router = _pad_cols(jnp.concatenate([moe_bg[layer], moe_br[layer]])[None, :], LANES)
        mods = [(mod(layer, 1, 0, p), mod(layer, 1, 1, p)) for p in range(2)]
        gates = [mod(layer, 1, 2, p) for p in range(2)]
        return _moe_sublayer(xin, mods, gates, ln_g[layer, 1].reshape(1, 1, D), ln_b[layer, 1].reshape(1, 1, D),
                             w_router, b_router, moe_w1[layer], moe_w3[layer], moe_w2[layer], cfgs)

    x2 = moe(0, x1)

    w_kv_b = w_kv.astype(BF16)
    kv = []
    for p in range(2):
        nb, tt = cfgs[p]
        kv_sh = kvm[rows[p], :D][:, None, :]
        kv_sc = kvm[rows[p], D:][:, None, :]
        kv.append(_modproj(x2[p], kv_sc, kv_sh, w_kv_b, nb, tt, F32))
    k_sh = [t[..., :KVW] for t in kv]
    v_sh = [t[..., KVW:] for t in kv]

    wq = at_wq[0].astype(BF16)
    wo1 = at_wo[0].astype(BF16)
    sinks = at_sinks[0].astype(F32)
    x3 = []
    for p in range(2):
        nb, tt = cfgs[p]
        sh, sc, gt = mod(1, 0, 0, p), mod(1, 0, 1, p), mod(1, 0, 2, p)
        q = _modproj(x2[p], sc, sh, wq, nb, tt, F32)
        if p == 0:
            o = _attn_prompt(sinks, q, _dup_heads(k_sh[p]), _dup_heads(v_sh[p]))
        else:
            k_all = jnp.concatenate([cache_k.reshape(BS, WINDOW, KVW), k_sh[p]], axis=1)
            v_all = jnp.concatenate([cache_v.reshape(BS, WINDOW, KVW), v_sh[p]], axis=1)
            o = _attn_sample(sinks, q, _dup_heads(k_all), _dup_heads(v_all))
        x3.append(_outproj_ln(o, x2[p], gt, ln_g[1, 0].reshape(1, 1, D), ln_b[1, 0].reshape(1, 1, D), wo1, nb, tt))

    x4 = moe(1, x3)

    k_p = k_sh[0][:, -WINDOW:].reshape(B, WINDOW, N_KV_HEADS, HEAD)
    v_p = v_sh[0][:, -WINDOW:].reshape(B, WINDOW, N_KV_HEADS, HEAD)
    k_s = k_sh[1].reshape(BS, TS, N_KV_HEADS, HEAD)
    v_s = v_sh[1].reshape(BS, TS, N_KV_HEADS, HEAD)
    return (x4[0], x4[1], new_wkv[0], new_shift[0], k_p, v_p, new_wkv[1], new_shift[1], k_s, v_s)
```

```python
import functools

import jax
import jax.numpy as jnp
from jax import lax
from jax.experimental import pallas as pl
from jax.experimental.pallas import tpu as pltpu

F32 = jnp.float32
BF16 = jnp.bfloat16

HEAD = 64
N_KV_HEADS = 4
WINDOW = 128
CHUNK = 64
N_GROUPS = 4
EXPERTS_PER_GROUP = 8
N_EXPERTS = N_GROUPS * EXPERTS_PER_GROUP
TOP_K = 2
DEPTH = 2
ALPHA = (2.0 * DEPTH) ** 0.25
LN_EPS = 1e-5
GN_EPS = 64e-5
NEG_INF = -1e30

LANES = 128
PAIR = LANES // HEAD

PROMPT_ROWS = 512
MOE_ROWS = 256
MODS_TN = 1024
WKV_PASSES = 1
WKV_SLAB = 128
ATTN_HEADS_PER_STAGE = 16
DMA_UNROLL = 8
VMEM_LIMIT = 56 * 1024 * 1024

NN = (((1,), (0,)), ((), ()))
NT = (((1,), (1,)), ((), ()))
TN = (((0,), (0,)), ((), ()))


def _mm(a, b, dims=NN, passes=1):
    if passes == 6:
        return lax.dot_general(a, b, dims, precision=lax.Precision.HIGHEST, preferred_element_type=F32)
    ah = a.astype(BF16)
    bh = b.astype(BF16)
    out = lax.dot_general(ah, bh, dims, preferred_element_type=F32)
    if passes == 3:
        al = (a - ah.astype(F32)).astype(BF16)
        bl = (b - bh.astype(F32)).astype(BF16)
        out = out + lax.dot_general(ah, bl, dims, preferred_element_type=F32)
        out = out + lax.dot_general(al, bh, dims, preferred_element_type=F32)
    return out


def _bf16_terms(x, terms):
    out = []
    for _ in range(terms):
        t = x.astype(BF16)
        out.append(t)
        x = x - t.astype(F32)
    return out


def _mm_mask(a, b, dims=NN, split="lhs", terms=2):
    if split == "lhs":
        bb = b.astype(BF16)
        parts = [lax.dot_general(t, bb, dims, preferred_element_type=F32) for t in _bf16_terms(a, terms)]
    else:
        aa = a.astype(BF16)
        parts = [lax.dot_general(aa, t, dims, preferred_element_type=F32) for t in _bf16_terms(b, terms)]
    return functools.reduce(lambda p, q: p + q, parts)


def _sigmoid(x):
    return 1.0 / (1.0 + jnp.exp(-x))


def _layer_norm(v, g, b):
    mu = jnp.mean(v, axis=-1, keepdims=True)
    d = v - mu
    var = jnp.mean(d * d, axis=-1, keepdims=True)
    return d * lax.rsqrt(var + LN_EPS) * g + b


def _params(*sem):
    return pltpu.CompilerParams(dimension_semantics=sem, vmem_limit_bytes=VMEM_LIMIT)


def _mods_kernel(c_ref, w_ref, b_ref, o_ref):
    c = c_ref[...]
    a = (c * _sigmoid(c)).astype(BF16)
    o_ref[0] = jnp.dot(a, w_ref[0].astype(BF16), preferred_element_type=F32) + b_ref[0]


def _mods(c, w, b):
    g, d, n = w.shape
    m = c.shape[0]
    tn = min(MODS_TN, n)
    assert n % tn == 0
    return pl.pallas_call(
        _mods_kernel,
        out_shape=jax.ShapeDtypeStruct((g, m, n), F32),
        grid=(g, n // tn),
        in_specs=[
            pl.BlockSpec((m, d), lambda i, j: (0, 0)),
            pl.BlockSpec((1, d, tn), lambda i, j: (i, 0, j)),
            pl.BlockSpec((1, 1, tn), lambda i, j: (i, 0, j)),
        ],
        out_specs=pl.BlockSpec((1, m, tn), lambda i, j: (i, 0, j)),
        compiler_params=_params("arbitrary", "arbitrary"),
        name="mods",
    )(c, w, b)


def _modulate(x_ref, sc_ref, sh_ref):
    nb, tt, d = x_ref.shape
    h = x_ref[...] * (1.0 + sc_ref[...]) + sh_ref[...]
    return h.reshape(nb * tt, d)


def _token_shift(h, prev_scr, prev0_ref, tt, t_axis):
    rows, d = h.shape
    nb = rows // tt

    @pl.when(pl.program_id(t_axis) == 0)
    def _():
        prev_scr[...] = prev0_ref[...]

    prev = jnp.broadcast_to(prev_scr[...], (nb, tt, d)).reshape(rows, d)
    rolled = pltpu.roll(h, 1, 0)
    row = lax.broadcasted_iota(jnp.int32, (rows, 1), 0)
    first = (row & (tt - 1)) == 0
    hp = jnp.where(first, prev, rolled)
    prev_scr[...] = h.reshape(nb, tt, d)[:, tt - 1:tt, :]
    return hp


def _rkv_kernel(x_ref, sc_ref, sh_ref, prev0_ref, mu_ref, w_ref, o_ref, prev_scr):
    nb, tt, d = x_ref.shape
    h = _modulate(x_ref, sc_ref, sh_ref)
    hp = _token_shift(h, prev_scr, prev0_ref, tt, 2)
    xin = h + (hp - h) * mu_ref[0]
    out = jnp.dot(xin.astype(BF16), w_ref[0], preferred_element_type=F32)
    o_ref[0] = out.reshape(nb, tt, out.shape[-1])


def _rkv(x, sc, sh, prev0, mu3, w3, nb, tt):
    NB, TT, D = x.shape
    seq = lambda j, b, t: (b, t, 0)
    per = lambda j, b, t: (b, 0, 0)
    return pl.pallas_call(
        _rkv_kernel,
        out_shape=jax.ShapeDtypeStruct((3, NB, TT, D), F32),
        grid=(3, NB // nb, TT // tt),
        in_specs=[
            pl.BlockSpec((nb, tt, D), seq),
            pl.BlockSpec((nb, 1, D), per),
            pl.BlockSpec((nb, 1, D), per),
            pl.BlockSpec((nb, 1, D), per),
            pl.BlockSpec((1, 1, D), lambda j, b, t: (j, 0, 0)),
            pl.BlockSpec((1, D, D), lambda j, b, t: (j, 0, 0)),
        ],
        out_specs=pl.BlockSpec((1, nb, tt, D), lambda j, b, t: (j, b, t, 0)),
        scratch_shapes=[pltpu.VMEM((nb, 1, D), F32)],
        compiler_params=_params("arbitrary", "arbitrary", "arbitrary"),
        name="tmix_rkv",
    )(x, sc, sh, prev0, mu3, w3)


def _lora_kernel(x_ref, sc_ref, sh_ref, prev0_ref, mu_ref, w1_ref, w2_ref, w0_ref, a1_ref, a2_ref, a0_ref,
                 g1_ref, g2_ref, lw_ref, a_ref, g_ref, last_ref, prev_scr):
    nb, tt, d = x_ref.shape
    h = _modulate(x_ref, sc_ref, sh_ref)
    hp = _token_shift(h, prev_scr, prev0_ref, tt, 1)
    dx = hp - h
    last_ref[...] = h.reshape(nb, tt, d)[:, tt - 1:tt, :]

    xw = (h + dx * mu_ref[0:1, :]).astype(BF16)
    zw = jnp.tanh(jnp.dot(xw, w1_ref[...], preferred_element_type=F32))
    zw = jnp.dot(zw.astype(BF16), w2_ref[...], preferred_element_type=F32) + w0_ref[...]
    sp = jnp.maximum(-zw, 0.0) + jnp.log1p(jnp.exp(-jnp.abs(zw)))
    lw_ref[...] = (-jnp.exp(-sp - 0.5)).reshape(nb, tt, d)

    xa = (h + dx * mu_ref[1:2, :]).astype(BF16)
    za = jnp.dot(xa, a1_ref[...], preferred_element_type=F32)
    za = jnp.dot(za.astype(BF16), a2_ref[...], preferred_element_type=F32) + a0_ref[...]
    a_ref[...] = _sigmoid(za).reshape(nb, tt, d)

    xg = (h + dx * mu_ref[2:3, :]).astype(BF16)
    zg = _sigmoid(jnp.dot(xg, g1_ref[...], preferred_element_type=F32))
    g_ref[...] = jnp.dot(zg.astype(BF16), g2_ref[...], preferred_element_type=F32).reshape(nb, tt, d)


def _lora(x, sc, sh, prev0, mu3, w1, w2, w0, a1, a2, a0, g1, g2, nb, tt):
    NB, TT, D = x.shape
    seq = lambda b, t: (b, t, 0)
    per = lambda b, t: (b, 0, 0)
    full = lambda a: pl.BlockSpec(a.shape, lambda b, t: (0,) * a.ndim)
    big = jax.ShapeDtypeStruct((NB, TT, D), F32)
    return pl.pallas_call(
        _lora_kernel,
        out_shape=(big, big, big, jax.ShapeDtypeStruct((NB, 1, D), F32)),
        grid=(NB // nb, TT // tt),
        in_specs=[
            pl.BlockSpec((nb, tt, D), seq),
            pl.BlockSpec((nb, 1, D), per),
            pl.BlockSpec((nb, 1, D), per),
            pl.BlockSpec((nb, 1, D), per),
            full(mu3), full(w1), full(w2), full(w0), full(a1), full(a2), full(a0), full(g1), full(g2),
        ],
        out_specs=(pl.BlockSpec((nb, tt, D), seq),) * 3 + (pl.BlockSpec((nb, 1, D), per),),
        scratch_shapes=[pltpu.VMEM((nb, 1, D), F32)],
        compiler_params=_params("arbitrary", "arbitrary"),
        name="tmix_lora",
    )(x, sc, sh, prev0, mu3, w1, w2, w0, a1, a2, a0, g1, g2)


def _wkv_kernel(r_ref, k_ref, v_ref, lw_ref, a_ref, g_ref, kkc_ref, kac_ref, rk_ref, lng_ref, lnb_ref, s0_ref,
                z_ref, sfin_ref, s_scr, *, passes):
    L = r_ref.shape[2]
    n_slabs, W = s_scr.shape[:2]
    G = W // HEAD
    c = pl.program_id(1)

    @pl.when(c == 0)
    def _():
        s_scr[...] = jnp.zeros(s_scr.shape, F32)
        for h in range(n_slabs * G):
            o = (h % G) * HEAD
            s_scr[h // G, o:o + HEAD, o:o + HEAD] = s0_ref[0, h]

    lane = lax.broadcasted_iota(jnp.int32, (1, W), 1)
    head_lanes = [(lane >= j * HEAD) & (lane < (j + 1) * HEAD) for j in range(G)]
    col = lax.broadcasted_iota(jnp.int32, (1, G * L), 1)
    head_cols = [(col >= j * L) & (col < (j + 1) * L) for j in range(G)]
    ri = lax.broadcasted_iota(jnp.int32, (L, G * L), 0)
    ci = lax.broadcasted_iota(jnp.int32, (L, G * L), 1) & (L - 1)
    strict = ri > ci
    incl = ri >= ci
    tril = jnp.where(incl[:, :L], 1.0, 0.0).astype(F32)
    bi = lax.broadcasted_iota(jnp.int32, (W, W), 0) // HEAD
    bj = lax.broadcasted_iota(jnp.int32, (W, W), 1) // HEAD
    same_head = bi == bj
    head_ones = jnp.where(same_head, 1.0, 0.0).astype(F32)

    def by_head(t, sel):
        return jnp.concatenate([jnp.where(s, t, 0.0) for s in sel], axis=0)

    slabs = list(range(n_slabs))
    sls = [slice(q * W, (q + 1) * W) for q in slabs]
    s0 = [s_scr[q] for q in slabs]
    rp = [r_ref[0, 0, :, sl] for sl in sls]
    kp = [k_ref[0, 0, :, sl] for sl in sls]
    vp = [v_ref[0, 0, :, sl] for sl in sls]
    lwp = [lw_ref[0, :, sl] for sl in sls]
    ap = [a_ref[0, :, sl] for sl in sls]
    cum = [_mm_mask(tril, t, split="rhs", terms=3) for t in lwp]
    kk0 = [k * kkc_ref[:, sl] for k, sl in zip(kp, sls)]
    n2 = [_mm_mask(t * t, head_ones, terms=1) for t in kk0]
    kkp = [t * lax.rsqrt(jnp.maximum(n, 1e-24)) for t, n in zip(kk0, n2)]
    khp = [k * (1.0 + (a - 1.0) * kac_ref[:, sl]) for k, a, sl in zip(kp, ap, sls)]
    c_incl = [jnp.exp(t) for t in cum]
    c_inv = [jnp.exp(-t) for t in cum]
    kt = [kk * jnp.exp(t - lw) for kk, t, lw in zip(kkp, cum, lwp)]
    rt = [r * ci_ for r, ci_ in zip(rp, c_incl)]
    kti = [kh * cv for kh, cv in zip(khp, c_inv)]
    bti = [kk * a * cv for kk, a, cv in zip(kkp, ap, c_inv)]
    lhs = [jnp.concatenate([a_, b_], axis=0) for a_, b_ in zip(kt, rt)]

    gk = [_mm(l_, by_head(t, head_lanes), NT, passes) for l_, t in zip(lhs, kti)]
    gb = [_mm(l_, by_head(t, head_lanes), NT, passes) for l_, t in zip(lhs, bti)]
    p0 = [_mm(l_, s_, NT, passes) for l_, s_ in zip(lhs, s0)]
    a_kk = [jnp.where(strict, t[:L], 0.0) for t in gk]
    a_rk = [jnp.where(incl, t[L:], 0.0) for t in gk]
    a_rb = [jnp.where(incl, t[L:], 0.0) for t in gb]
    pw = [jnp.where(strict, -t[:L], 0.0) for t in gb]
    sv = [by_head(t, head_lanes) for t in vp]
    x = [p[:L] + _mm(a_, s_, NN, passes) for p, a_, s_ in zip(p0, a_kk, sv)]
    u = x
    n = 1
    while n < L:
        if n * 2 >= L:
            u = [u_ + _mm(p_, by_head(u_, head_lanes), NN, passes) for p_, u_ in zip(pw, u)]
        elif (G * L) % LANES == 0:
            both = [_mm(p_, jnp.concatenate([by_head(p_, head_cols), by_head(u_, head_lanes)], axis=1), NN, passes)
                    for p_, u_ in zip(pw, u)]
            pw = [t[:, :G * L] for t in both]
            u = [u_ + t[:, G * L:] for u_, t in zip(u, both)]
        else:
            u = [u_ + _mm(p_, by_head(u_, head_lanes), NN, passes) for p_, u_ in zip(pw, u)]
            pw = [_mm(p_, by_head(p_, head_cols), NN, passes) for p_ in pw]
        n *= 2
    y = [p[L:] + _mm(ak, s_, NN, passes) - _mm(ab, by_head(u_, head_lanes), NN, passes)
         for p, ak, s_, ab, u_ in zip(p0, a_rk, sv, a_rb, u)]
    ds = [_mm(jnp.concatenate([v_, u_], axis=0), jnp.concatenate([kt_, -bt_], axis=0), TN, passes)
          for v_, u_, kt_, bt_ in zip(vp, u, kti, bti)]
    s_new = [(s_ + jnp.where(same_head, d_, 0.0)) * ci_[L - 1:L, :] for s_, d_, ci_ in zip(s0, ds, c_incl)]

    mean = [_mm_mask(t, head_ones, terms=1) * (1.0 / HEAD) for t in y]
    dlt = [t - m_ for t, m_ in zip(y, mean)]
    var = [_mm_mask(t * t, head_ones, terms=1) * (1.0 / HEAD) for t in dlt]
    bonus = [_mm_mask(r * kh * rk_ref[:, sl], head_ones, terms=1) * v for r, kh, v, sl in zip(rp, khp, vp, sls)]
    for q, sl in zip(slabs, sls):
        yn = dlt[q] * lax.rsqrt(var[q] + GN_EPS) * lng_ref[:, sl] + lnb_ref[:, sl]
        z_ref[0, :, sl] = ((yn + bonus[q]) * g_ref[0, :, sl]).astype(z_ref.dtype)
    for q in slabs:
        s_scr[q] = s_new[q]

    @pl.when(c == pl.num_programs(1) - 1)
    def _():
        for h in range(n_slabs * G):
            o = (h % G) * HEAD
            sfin_ref[0, h] = s_scr[h // G, o:o + HEAD, o:o + HEAD]


def _wkv(rkv, lw, a, g, kkc, kac, rk, lng, lnb, s0, L, passes=WKV_PASSES):
    _, NB, TT, D = rkv.shape
    H = D // HEAD
    seq = lambda b, t: (b, t, 0)
    vec = pl.BlockSpec((1, D), lambda b, t: (0, 0))
    st = pl.BlockSpec((1, H, HEAD, HEAD), lambda b, t: (b, 0, 0, 0))
    return pl.pallas_call(
        functools.partial(_wkv_kernel, passes=passes),
        out_shape=(jax.ShapeDtypeStruct((NB, TT, D), BF16), jax.ShapeDtypeStruct((NB, H, HEAD, HEAD), F32)),
        grid=(NB, TT // L),
        in_specs=[
            pl.BlockSpec((1, 1, L, D), lambda b, t: (0, b, t, 0)),
            pl.BlockSpec((1, 1, L, D), lambda b, t: (1, b, t, 0)),
            pl.BlockSpec((1, 1, L, D), lambda b, t: (2, b, t, 0)),
            pl.BlockSpec((1, L, D), seq),
            pl.BlockSpec((1, L, D), seq),
            pl.BlockSpec((1, L, D), seq),
            vec, vec, vec, vec, vec, st,
        ],
        out_specs=(pl.BlockSpec((1, L, D), seq), st),
        scratch_shapes=[pltpu.VMEM((H * HEAD // WKV_SLAB, WKV_SLAB, WKV_SLAB), F32)],
        compiler_params=_params("arbitrary", "arbitrary"),
        name="wkv",
    )(rkv, rkv, rkv, lw, a, g, kkc, kac, rk, lng, lnb, s0)


def _outproj_ln_kernel(a_ref, x_ref, gate_ref, lng_ref, lnb_ref, w_ref, o_ref):
    nb, tt, d = x_ref.shape
    a = a_ref[...].reshape(nb * tt, a_ref.shape[-1])
    y = jnp.dot(a, w_ref[...], preferred_element_type=F32).reshape(nb, tt, d)
    v = ALPHA * x_ref[...] + (1.0 + gate_ref[...]) * y
    o_ref[...] = _layer_norm(v, lng_ref[...], lnb_ref[...])


def _outproj_ln(a, x, gate, lng, lnb, w, nb, tt):
    NB, TT, D = x.shape
    seq = lambda b, t: (b, t, 0)
    per = lambda b, t: (b, 0, 0)
    vec = pl.BlockSpec((1, 1, D), lambda b, t: (0, 0, 0))
    return pl.pallas_call(
        _outproj_ln_kernel,
        out_shape=jax.ShapeDtypeStruct((NB, TT, D), F32),
        grid=(NB // nb, TT // tt),
        in_specs=[
            pl.BlockSpec((nb, tt, a.shape[-1]), seq),
            pl.BlockSpec((nb, tt, D), seq),
            pl.BlockSpec((nb, 1, D), per),
            vec, vec,
            pl.BlockSpec(w.shape, lambda b, t: (0, 0)),
        ],
        out_specs=pl.BlockSpec((nb, tt, D), seq),
        compiler_params=_params("arbitrary", "arbitrary"),
        name="outproj_ln",
    )(a, x, gate, lng, lnb, w)


def _modproj_kernel(x_ref, sc_ref, sh_ref, *refs):
    n = len(refs) // 2
    nb, tt, d = x_ref.shape
    h = _modulate(x_ref, sc_ref, sh_ref).astype(BF16)
    for w_ref, o_ref in zip(refs[:n], refs[n:]):
        out = jnp.dot(h, w_ref[...], preferred_element_type=F32)
        o_ref[...] = out.reshape(nb, tt, out.shape[-1]).astype(o_ref.dtype)


def _modproj(x, sc, sh, ws, nb, tt, out_dtypes):
    NB, TT, D = x.shape
    seq = lambda b, t: (b, t, 0)
    per = lambda b, t: (b, 0, 0)
    return pl.pallas_call(
        _modproj_kernel,
        out_shape=tuple(jax.ShapeDtypeStruct((NB, TT, w.shape[1]), dt) for w, dt in zip(ws, out_dtypes)),
        grid=(NB // nb, TT // tt),
        in_specs=[
            pl.BlockSpec((nb, tt, D), seq),
            pl.BlockSpec((nb, 1, D), per),
            pl.BlockSpec((nb, 1, D), per),
        ] + [pl.BlockSpec(w.shape, lambda b, t: (0, 0)) for w in ws],
        out_specs=tuple(pl.BlockSpec((nb, tt, w.shape[1]), seq) for w in ws),
        compiler_params=_params("arbitrary", "arbitrary"),
        name="modproj",
    )(x, sc, sh, *ws)


ROUTE_E, ROUTE_RANK, ROUTE_GATE = 0, 2, 4
BIG_NEG = -3.0e38


def _route_kernel(x_ref, sc_ref, sh_ref, w_ref, b_ref, cnt0_ref, route_ref, cnt_ref, cnt_scr):
    first = (pl.program_id(0) == 0) & (pl.program_id(1) == 0)

    @pl.when(first)
    def _():
        cnt_scr[...] = cnt0_ref[...]

    h = _modulate(x_ref, sc_ref, sh_ref)
    rows = h.shape[0]
    lg = _mm(h, w_ref[...], NN, passes=3) + b_ref[...]
    lane = lax.broadcasted_iota(jnp.int32, (1, LANES), 1)

    def first_argmax(v, vmax):
        return jnp.min(jnp.where(v == vmax, lane, LANES), axis=-1, keepdims=True)

    is_g = lane < N_GROUPS
    gl = jnp.where(is_g, lg, BIG_NEG)
    gmax = jnp.max(gl, axis=-1, keepdims=True)
    g_sel = first_argmax(gl, gmax)
    g_prob = 1.0 / jnp.sum(jnp.where(is_g, jnp.exp(lg - gmax), 0.0), axis=-1, keepdims=True)
    lo = N_GROUPS + EXPERTS_PER_GROUP * g_sel
    el = jnp.where((lane >= lo) & (lane < lo + EXPERTS_PER_GROUP), lg, BIG_NEG)
    v1 = jnp.max(el, axis=-1, keepdims=True)
    i1 = first_argmax(el, v1)
    el2 = jnp.where(lane == i1, BIG_NEG, el)
    v2 = jnp.max(el2, axis=-1, keepdims=True)
    i2 = first_argmax(el2, v2)
    e1 = i1 - N_GROUPS
    e2 = i2 - N_GROUPS
    t = jnp.exp(v2 - v1)
    p1 = 1.0 / (1.0 + t)
    gate1 = p1 * g_prob
    gate2 = t * p1 * g_prob

    oh1 = jnp.where(lane == e1, 1.0, 0.0)
    oh2 = jnp.where(lane == e2, 1.0, 0.0)
    ri = lax.broadcasted_iota(jnp.int32, (rows, rows), 0)
    ci = lax.broadcasted_iota(jnp.int32, (rows, rows), 1)
    before = jnp.where(ri > ci, 1.0, 0.0).astype(BF16)
    c1 = jnp.dot(before, oh1.astype(BF16), preferred_element_type=F32)
    c2 = jnp.dot(before, oh2.astype(BF16), preferred_element_type=F32)
    tot1 = jnp.sum(oh1, axis=0, keepdims=True)
    tot2 = jnp.sum(oh2, axis=0, keepdims=True)
    base = cnt_scr[...]
    rank1 = jnp.sum(oh1 * (base + c1), axis=-1, keepdims=True)
    rank2 = jnp.sum(oh2 * (base + tot1 + c2), axis=-1, keepdims=True)
    cnt_new = base + tot1 + tot2
    cnt_scr[...] = cnt_new
    cnt_ref[...] = cnt_new

    out = jnp.where(lane == ROUTE_E, e1.astype(F32), 0.0)
    out = jnp.where(lane == ROUTE_E + 1, e2.astype(F32), out)
    out = jnp.where(lane == ROUTE_RANK, rank1, out)
    out = jnp.where(lane == ROUTE_RANK + 1, rank2, out)
    out = jnp.where(lane == ROUTE_GATE, gate1, out)
    out = jnp.where(lane == ROUTE_GATE + 1, gate2, out)
    route_ref[...] = out


def _route(x, sc, sh, w, b, cnt0, nb, tt):
    NB, TT, D = x.shape
    steps_t = TT // tt
    seq = lambda b_, t: (b_, t, 0)
    per = lambda b_, t: (b_, 0, 0)
    one = lambda b_, t: (0, 0)
    return pl.pallas_call(
        _route_kernel,
        out_shape=(jax.ShapeDtypeStruct((NB * TT, LANES), F32), jax.ShapeDtypeStruct((1, LANES), F32)),
        grid=(NB // nb, steps_t),
        in_specs=[
            pl.BlockSpec((nb, tt, D), seq),
            pl.BlockSpec((nb, 1, D), per),
            pl.BlockSpec((nb, 1, D), per),
            pl.BlockSpec(w.shape, one),
            pl.BlockSpec(b.shape, one),
            pl.BlockSpec((1, LANES), one),
        ],
        out_specs=(pl.BlockSpec((nb * tt, LANES), lambda b_, t: (b_ * steps_t + t, 0)),
                   pl.BlockSpec((1, LANES), one)),
        scratch_shapes=[pltpu.VMEM((1, LANES), F32)],
        compiler_params=_params("arbitrary", "arbitrary"),
        name="route",
    )(x, sc, sh, w, b, cnt0)


def _row_copy(src, dst, sem):
    return pltpu.make_async_copy(src, dst, sem)


def _dispatch_kernel(dest_ref, x_ref, sc_ref, sh_ref, xb_in_ref, xb_ref, buf, sems):
    del xb_in_ref
    steps_t = pl.num_programs(1)
    i = pl.program_id(0) * steps_t + pl.program_id(1)
    n_steps = pl.num_programs(0) * steps_t
    slot = i % 2
    rows, half = buf.shape[1:]

    def wait_slot(s):
        for _ in range(TOP_K):
            _row_copy(buf.at[s], xb_ref.at[pl.ds(0, rows)], sems.at[s]).wait()

    @pl.when(i >= 2)
    def _():
        wait_slot(slot)

    h = _modulate(x_ref, sc_ref, sh_ref)
    bits = pltpu.bitcast(h.astype(BF16).astype(F32), jnp.uint32)
    buf[slot] = bits[:, :half] | (bits[:, half:] >> 16)

    def issue(r, carry):
        for j in range(TOP_K):
            d = dest_ref[TOP_K * r + j]
            _row_copy(buf.at[slot, pl.ds(r, 1)], xb_ref.at[pl.ds(d, 1)], sems.at[slot]).start()
        return carry
    lax.fori_loop(0, rows, issue, 0, unroll=DMA_UNROLL)

    @pl.when(i == n_steps - 1)
    def _():
        wait_slot(slot)

    @pl.when((i == n_steps - 1) & (i >= 1))
    def _():
        wait_slot(1 - slot)


def _moe_dispatch(dest_flat, x, sc, sh, xb, nb, tt):
    NB, TT, D = x.shape
    steps_t = TT // tt
    rows = nb * tt
    seq = lambda b_, t: (b_, t, 0)
    per = lambda b_, t: (b_, 0, 0)
    return pl.pallas_call(
        _dispatch_kernel,
        out_shape=jax.ShapeDtypeStruct(xb.shape, xb.dtype),
        grid=(NB // nb, steps_t),
        in_specs=[
            pl.BlockSpec((rows * TOP_K,), lambda b_, t: (b_ * steps_t + t,), memory_space=pltpu.SMEM),
            pl.BlockSpec((nb, tt, D), seq),
            pl.BlockSpec((nb, 1, D), per),
            pl.BlockSpec((nb, 1, D), per),
            pl.BlockSpec(memory_space=pl.ANY),
        ],
        out_specs=pl.BlockSpec(memory_space=pl.ANY),
        scratch_shapes=[pltpu.VMEM((2, rows, D // 2), jnp.uint32), pltpu.SemaphoreType.DMA((2,))],
        input_output_aliases={4: 0},
        compiler_params=_params("arbitrary", "arbitrary"),
        name="moe_dispatch",
    )(dest_flat, x, sc, sh, xb)


def _moe_kernel(be_ref, used_ref, x_ref, w1_ref, w3_ref, w2_ref, o_ref, w1s, w3s, w2s):
    i = pl.program_id(0)
    e = be_ref[i]
    e_prev = be_ref[jnp.maximum(i - 1, 0)]

    @pl.when((i == 0) | (e != e_prev))
    def _():
        w1s[...] = w1_ref[0].astype(BF16)
        w3s[...] = w3_ref[0].astype(BF16)
        w2s[...] = w2_ref[0].astype(BF16)

    @pl.when(i < used_ref[0])
    def _():
        p = x_ref[...]
        hi = pltpu.bitcast(p & jnp.uint32(0xFFFF0000), F32)
        lo = pltpu.bitcast(p << 16, F32)
        x = jnp.concatenate([hi, lo], axis=1).astype(BF16)
        h1 = jnp.dot(x, w1s[...], preferred_element_type=F32)
        h3 = jnp.dot(x, w3s[...], preferred_element_type=F32)
        act = (h1 * _sigmoid(h1) * h3).astype(BF16)
        o_ref[...] = jnp.dot(act, w2s[...], preferred_element_type=F32)

    @pl.when(i >= used_ref[0])
    def _():
        o_ref[...] = jnp.zeros(o_ref.shape, F32)


def _moe_ffn(blk_expert, n_used, xb, w1, w3, w2):
    rows, half = xb.shape
    E, D, DE = w1.shape
    n_blocks = rows // MOE_ROWS
    return pl.pallas_call(
        _moe_kernel,
        out_shape=jax.ShapeDtypeStruct((rows, D), F32),
        grid_spec=pltpu.PrefetchScalarGridSpec(
            num_scalar_prefetch=2,
            grid=(n_blocks,),
            in_specs=[
                pl.BlockSpec((MOE_ROWS, half), lambda i, be, nu: (i, 0)),
                pl.BlockSpec((1, D, DE), lambda i, be, nu: (be[i], 0, 0)),
                pl.BlockSpec((1, D, DE), lambda i, be, nu: (be[i], 0, 0)),
                pl.BlockSpec((1, DE, D), lambda i, be, nu: (be[i], 0, 0)),
            ],
            out_specs=pl.BlockSpec((MOE_ROWS, D), lambda i, be, nu: (i, 0)),
            scratch_shapes=[pltpu.VMEM((D, DE), BF16), pltpu.VMEM((D, DE), BF16), pltpu.VMEM((DE, D), BF16)],
        ),
        compiler_params=_params("arbitrary"),
        name="moe_ffn",
    )(blk_expert, n_used, xb, w1, w3, w2)


def _combine_ln_kernel(dest_ref, dest_next_ref, x_ref, route_ref, gate_ref, lng_ref, lnb_ref, yb_ref, o_ref,
                       ybuf, sems):
    steps_t = pl.num_programs(1)
    i = pl.program_id(0) * steps_t + pl.program_id(1)
    n_steps = pl.num_programs(0) * steps_t
    slot = i % 2
    nb, tt, d = x_ref.shape
    rows = nb * tt

    def fetch(d_ref, s):
        def body(r, carry):
            for j in range(TOP_K):
                row = d_ref[TOP_K * r + j]
                _row_copy(yb_ref.at[pl.ds(row, 1)], ybuf.at[s, j, pl.ds(r, 1)], sems.at[s]).start()
            return carry
        lax.fori_loop(0, rows, body, 0, unroll=DMA_UNROLL)

    @pl.when(i == 0)
    def _():
        fetch(dest_ref, slot)

    for j in range(TOP_K):
        _row_copy(yb_ref.at[pl.ds(0, rows)], ybuf.at[slot, j], sems.at[slot]).wait()

    @pl.when(i + 1 < n_steps)
    def _():
        fetch(dest_next_ref, 1 - slot)

    rt = route_ref[...]
    y = rt[:, ROUTE_GATE:ROUTE_GATE + 1] * ybuf[slot, 0] + rt[:, ROUTE_GATE + 1:ROUTE_GATE + 2] * ybuf[slot, 1]
    v = ALPHA * x_ref[...] + (1.0 + gate_ref[...]) * y.reshape(nb, tt, d)
    o_ref[...] = _layer_norm(v, lng_ref[...], lnb_ref[...])


def _combine_ln(dest_flat, x, route, gate, lng, lnb, yb, nb, tt, row0):
    NB, TT, D = x.shape
    steps_t = TT // tt
    n_steps = (NB // nb) * steps_t
    rows = nb * tt
    blk0 = row0 // rows
    seq = lambda b_, t: (b_, t, 0)
    per = lambda b_, t: (b_, 0, 0)
    vec = pl.BlockSpec((1, 1, D), lambda b_, t: (0, 0, 0))
    cur = lambda b_, t: (blk0 + b_ * steps_t + t,)
    nxt = lambda b_, t: (blk0 + jnp.minimum(b_ * steps_t + t + 1, n_steps - 1),)
    return pl.pallas_call(
        _combine_ln_kernel,
        out_shape=jax.ShapeDtypeStruct((NB, TT, D), F32),
        grid=(NB // nb, steps_t),
        in_specs=[
            pl.BlockSpec((rows * TOP_K,), cur, memory_space=pltpu.SMEM),
            pl.BlockSpec((rows * TOP_K,), nxt, memory_space=pltpu.SMEM),
            pl.BlockSpec((nb, tt, D), seq),
            pl.BlockSpec((rows, LANES), lambda b_, t: (blk0 + b_ * steps_t + t, 0)),
            pl.BlockSpec((nb, 1, D), per),
            vec, vec,
            pl.BlockSpec(memory_space=pl.ANY),
        ],
        out_specs=pl.BlockSpec((nb, tt, D), seq),
        scratch_shapes=[pltpu.VMEM((2, TOP_K, rows, D), F32), pltpu.SemaphoreType.DMA((2,))],
        compiler_params=_params("arbitrary", "arbitrary"),
        name="moe_combine_ln",
    )(dest_flat, dest_flat, x, route, gate, lng, lnb, yb)


def _moe_sublayer(xs, mods, gates, lng, lnb, w_router, b_router, w1, w3, w2, cfgs):
    D = xs[0].shape[-1]
    routes = []
    cnt = jnp.zeros((1, LANES), F32)
    for x, (sh, sc), (nb, tt) in zip(xs, mods, cfgs):
        route, cnt = _route(x, sc, sh, w_router, b_router, cnt, nb, tt)
        routes.append(route)
    route = jnp.concatenate(routes, axis=0)
    n = route.shape[0]
    counts = cnt[0, :N_EXPERTS].astype(jnp.int32)
    padded = (counts + MOE_ROWS - 1) // MOE_ROWS * MOE_ROWS
    pend = jnp.cumsum(padded)
    pstart = pend - padded
    eid = route[:, ROUTE_E:ROUTE_E + TOP_K].astype(jnp.int32)
    rank = route[:, ROUTE_RANK:ROUTE_RANK + TOP_K].astype(jnp.int32)
    experts = jnp.arange(N_EXPERTS, dtype=jnp.int32)
    dest = (jnp.sum(jnp.where(eid[..., None] == experts, pstart, 0), axis=-1) + rank).reshape(n * TOP_K)
    n_blocks = -(-n * TOP_K // MOE_ROWS) + N_EXPERTS
    blk_start = jnp.arange(n_blocks, dtype=jnp.int32) * MOE_ROWS
    blk_expert = jnp.minimum(jnp.sum((blk_start[:, None] >= pend[None, :]).astype(jnp.int32), axis=1), N_EXPERTS - 1)
    n_used = (pend[-1] // MOE_ROWS).astype(jnp.int32).reshape(1)

    xb = jnp.zeros((n_blocks * MOE_ROWS, D // 2), jnp.uint32)
    row0s = []
    row = 0
    for x, (sh, sc), (nb, tt) in zip(xs, mods, cfgs):
        rows_p = x.shape[0] * x.shape[1]
        xb = _moe_dispatch(lax.dynamic_slice_in_dim(dest, row * TOP_K, rows_p * TOP_K), x, sc, sh, xb, nb, tt)
        row0s.append(row)
        row += rows_p
    yb = _moe_ffn(blk_expert, n_used, xb, w1, w3, w2)
    return [_combine_ln(dest, x, route, gt, lng, lnb, yb, nb, tt, row0)
            for x, gt, (nb, tt), row0 in zip(xs, gates, cfgs, row0s)]


def _attn_kernel(sinks_ref, q_ref, *refs, n_band, first_key_chunk):
    k_refs = refs[:n_band]
    v_refs = refs[n_band:2 * n_band]
    o_ref = refs[2 * n_band]
    tq = q_ref.shape[1]
    kd = jnp.concatenate([r[0] for r in k_refs], axis=0) if n_band > 1 else k_refs[0][0]
    vd = jnp.concatenate([r[0] for r in v_refs], axis=0) if n_band > 1 else v_refs[0][0]
    nk = kd.shape[0]
    qi = lax.broadcasted_iota(jnp.int32, (tq, nk), 0)
    kj = lax.broadcasted_iota(jnp.int32, (tq, nk), 1)
    dist = jnp.abs((nk - tq) + qi - kj).astype(F32)
    if first_key_chunk is not None:
        k_pos = (pl.program_id(1) + first_key_chunk) * tq + kj
        valid = k_pos >= 0
    lane = lax.broadcasted_iota(jnp.int32, (1, LANES), 1)
    n_q_heads = q_ref.shape[2] // HEAD
    q_per_kv = n_q_heads // N_KV_HEADS
    scale = HEAD ** -0.5
    k_half, v_half = [], []
    for g in range(N_KV_HEADS):
        kg = kd[:, g * LANES:(g + 1) * LANES]
        vg = vd[:, g * LANES:(g + 1) * LANES]
        zero = jnp.zeros_like(kg)
        k_half.append([jnp.where(lane < HEAD, kg, zero), jnp.where(lane >= HEAD, kg, zero)])
        v_half.append([jnp.where(lane < HEAD, vg, zero), jnp.where(lane >= HEAD, vg, zero)])
    per_stage = min(ATTN_HEADS_PER_STAGE, n_q_heads)
    for h0 in range(0, n_q_heads, per_stage):
        hs = list(range(h0, h0 + per_stage))
        qp = {h // PAIR: q_ref[0, :, (h // PAIR) * LANES:(h // PAIR + 1) * LANES].astype(BF16) for h in hs}
        s = [lax.dot_general(qp[h // PAIR], k_half[h // q_per_kv][h % PAIR], NT, preferred_element_type=F32)
             for h in hs]
        s = [t * scale - (2.0 ** (-8.0 * (h + 1) / n_q_heads)) * dist for t, h in zip(s, hs)]
        if first_key_chunk is not None:
            s = [jnp.where(valid, t, NEG_INF) for t in s]
        mx = [jnp.maximum(jnp.max(t, axis=-1, keepdims=True), sinks_ref[h]) for t, h in zip(s, hs)]
        pr = [jnp.exp(t - m) for t, m in zip(s, mx)]
        den = [jnp.sum(t, axis=-1, keepdims=True) + jnp.exp(sinks_ref[h] - m) for t, m, h in zip(pr, mx, hs)]
        pr = [(t / d).astype(BF16) for t, d in zip(pr, den)]
        o = [jnp.dot(t, v_half[h // q_per_kv][h % PAIR], preferred_element_type=F32) for t, h in zip(pr, hs)]
        for j in range(0, len(hs), PAIR):
            pair = hs[j] // PAIR
            o_ref[0, :, pair * LANES:(pair + 1) * LANES] = (o[j] + o[j + 1]).astype(o_ref.dtype)


def _attn_prompt(sinks, q, kd, vd):
    B, T, D = q.shape
    nc = T // CHUNK
    w_chunks = WINDOW // CHUNK
    n_band = w_chunks + 1
    KW = kd.shape[-1]
    band = [pl.BlockSpec((1, CHUNK, KW), (lambda b, c, s, j=j: (b, jnp.maximum(c - w_chunks + j, 0), 0)))
            for j in range(n_band)]
    return pl.pallas_call(
        functools.partial(_attn_kernel, n_band=n_band, first_key_chunk=-w_chunks),
        out_shape=jax.ShapeDtypeStruct((B, T, D), BF16),
        grid_spec=pltpu.PrefetchScalarGridSpec(
            num_scalar_prefetch=1,
            grid=(B, nc),
            in_specs=[pl.BlockSpec((1, CHUNK, D), lambda b, c, s: (b, c, 0))] + band + band,
            out_specs=pl.BlockSpec((1, CHUNK, D), lambda b, c, s: (b, c, 0)),
        ),
        compiler_params=_params("arbitrary", "arbitrary"),
        name="attn_prompt",
    )(sinks, q, *([kd] * n_band), *([vd] * n_band))


def _attn_sample(sinks, q, kd, vd):
    B, T, D = q.shape
    NK, KW = kd.shape[1:]
    return pl.pallas_call(
        functools.partial(_attn_kernel, n_band=1, first_key_chunk=None),
        out_shape=jax.ShapeDtypeStruct((B, T, D), BF16),
        grid_spec=pltpu.PrefetchScalarGridSpec(
            num_scalar_prefetch=1,
            grid=(B, 1),
            in_specs=[pl.BlockSpec((1, T, D), lambda b, c, s: (b, 0, 0)),
                      pl.BlockSpec((1, NK, KW), lambda b, c, s: (b, 0, 0)),
                      pl.BlockSpec((1, NK, KW), lambda b, c, s: (b, 0, 0))],
            out_specs=pl.BlockSpec((1, T, D), lambda b, c, s: (b, 0, 0)),
        ),
        compiler_params=_params("arbitrary", "arbitrary"),
        name="attn_sample",
    )(sinks, q, kd, vd)


def _dup_heads(t):
    B, T, _ = t.shape
    t4 = t.reshape(B, T, N_KV_HEADS, 1, HEAD)
    return jnp.broadcast_to(t4, (B, T, N_KV_HEADS, PAIR, HEAD)).reshape(B, T, N_KV_HEADS * LANES).astype(BF16)


def _pad_cols(w, n):
    return jnp.pad(w, ((0, 0), (0, n - w.shape[1])))


def _pad_rows(w, n):
    return jnp.pad(w, ((0, n - w.shape[0]), (0, 0)))


def kernel(x_prompt, x_sample, c_prompt, c_sample, state_wkv, state_shift, cache_k, cache_v, mod_w, mod_b, ln_g, ln_b, rw_mu, rw_wr, rw_wk, rw_wv, rw_w0, rw_w1, rw_w2, rw_a0, rw_a1, rw_a2, rw_g1, rw_g2, rw_kk, rw_ka, rw_rk, rw_lnx_g, rw_lnx_b, rw_wo, kv_mod_w, kv_mod_b, w_kv, at_wq, at_sinks, at_wo, moe_wg, moe_bg, moe_wr, moe_br, moe_w1, moe_w3, moe_w2):
    B, T, D = x_prompt.shape
    BS, TS, _ = x_sample.shape
    H = D // HEAD
    KVW = N_KV_HEADS * HEAD
    cfgs = [(1, min(PROMPT_ROWS, T)), (BS, TS)]
    chunk_len = [CHUNK, TS]

    c_all = jnp.concatenate([c_prompt, c_sample], axis=0)
    m_all = _mods(c_all, mod_w.reshape(DEPTH * 2, D, 3 * D), mod_b.reshape(DEPTH * 2, 1, 3 * D))
    kvm = _mods(c_all, kv_mod_w[None], kv_mod_b[None, None])[0]
    rows = [slice(0, B), slice(B, B + BS)]

    def mod(layer, sub, part, path):
        return m_all[layer * 2 + sub, rows[path], part * D:(part + 1) * D][:, None, :]

    xs = [x_prompt, x_sample]
    shift0 = [jnp.zeros((B, 1, D), F32), state_shift[0][:, None, :]]
    wkv0 = [jnp.zeros((B, H, HEAD, HEAD), F32), state_wkv[0]]

    mu = rw_mu[0]
    mu_rkv = jnp.stack([mu[0], mu[2], mu[3]])[:, None, :]
    mu_lora = jnp.stack([mu[1], mu[4], mu[5]])
    w_rkv = jnp.stack([rw_wr[0], rw_wk[0], rw_wv[0]]).astype(BF16)
    lw1 = _pad_cols(rw_w1[0], LANES).astype(BF16)
    lw2 = _pad_rows(rw_w2[0], LANES).astype(BF16)
    la1 = _pad_cols(rw_a1[0], LANES).astype(BF16)
    la2 = _pad_rows(rw_a2[0], LANES).astype(BF16)
    lg1 = rw_g1[0].astype(BF16)
    lg2 = rw_g2[0].astype(BF16)
    wo = rw_wo[0].astype(BF16)
    vec = lambda v: v.reshape(1, D)
    new_wkv, new_shift, x1 = [], [], []
    for p in range(2):
        nb, tt = cfgs[p]
        sh, sc, gt = mod(0, 0, 0, p), mod(0, 0, 1, p), mod(0, 0, 2, p)
        rkv = _rkv(xs[p], sc, sh, shift0[p], mu_rkv, w_rkv, nb, tt)
        lw, a, g, last = _lora(xs[p], sc, sh, shift0[p], mu_lora, lw1, lw2, vec(rw_w0[0]), la1, la2, vec(rw_a0[0]),
                               lg1, lg2, nb, tt)
        z, s_fin = _wkv(rkv, lw, a, g, vec(rw_kk[0]), vec(rw_ka[0]), vec(rw_rk[0]), vec(rw_lnx_g[0]),
                        vec(rw_lnx_b[0]), wkv0[p], chunk_len[p])
        x1.append(_outproj_ln(z, xs[p], gt, ln_g[0, 0].reshape(1, 1, D), ln_b[0, 0].reshape(1, 1, D), wo, nb, tt))
        new_wkv.append(s_fin[None])
        new_shift.append(last.reshape(1, -1, D))

    def moe(layer, xin):
        w_router = _pad_cols(jnp.concatenate([moe_wg[layer], moe_wr[layer]], axis=1), LANES)
        b_router = _pad_cols(jnp.concatenate([moe_bg[layer], moe_br[layer]])[None, :], LANES)
        mods = [(mod(layer, 1, 0, p), mod(layer, 1, 1, p)) for p in range(2)]
        gates = [mod(layer, 1, 2, p) for p in range(2)]
        return _moe_sublayer(xin, mods, gates, ln_g[layer, 1].reshape(1, 1, D), ln_b[layer, 1].reshape(1, 1, D),
                             w_router, b_router, moe_w1[layer], moe_w3[layer], moe_w2[layer], cfgs)

    x2 = moe(0, x1)

    w_kv_b = w_kv.astype(BF16)
    w_kd, w_vd = (_dup_heads(w_kv_b[None, :, i * KVW:(i + 1) * KVW])[0] for i in range(2))
    kv = []
    for p in range(2):
        nb, tt = cfgs[p]
        kv_sh = kvm[rows[p], :D][:, None, :]
        kv_sc = kvm[rows[p], D:][:, None, :]
        kv.append(_modproj(x2[p], kv_sc, kv_sh, (w_kv_b, w_kd, w_vd), nb, tt, (F32, BF16, BF16)))
    k_sh = [t[0][..., :KVW] for t in kv]
    v_sh = [t[0][..., KVW:] for t in kv]

    wq = at_wq[0].astype(BF16)
    wo1 = at_wo[0].astype(BF16)
    sinks = at_sinks[0].astype(F32)
    x3 = []
    for p in range(2):
        nb, tt = cfgs[p]
        sh, sc, gt = mod(1, 0, 0, p), mod(1, 0, 1, p), mod(1, 0, 2, p)
        q, = _modproj(x2[p], sc, sh, (wq,), nb, tt, (F32,))
        if p == 0:
            o = _attn_prompt(sinks, q, kv[p][1], kv[p][2])
        else:
            k_all = jnp.concatenate([cache_k.reshape(BS, WINDOW, KVW), k_sh[p]], axis=1)
            v_all = jnp.concatenate([cache_v.reshape(BS, WINDOW, KVW), v_sh[p]], axis=1)
            o = _attn_sample(sinks, q, _dup_heads(k_all), _dup_heads(v_all))
        x3.append(_outproj_ln(o, x2[p], gt, ln_g[1, 0].reshape(1, 1, D), ln_b[1, 0].reshape(1, 1, D), wo1, nb, tt))

    x4 = moe(1, x3)

    k_p = k_sh[0][:, -WINDOW:].reshape(B, WINDOW, N_KV_HEADS, HEAD)
    v_p = v_sh[0][:, -WINDOW:].reshape(B, WINDOW, N_KV_HEADS, HEAD)
    k_s = k_sh[1].reshape(BS, TS, N_KV_HEADS, HEAD)
    v_s = v_sh[1].reshape(BS, TS, N_KV_HEADS, HEAD)
    return (x4[0], x4[1], new_wkv[0], new_shift[0], k_p, v_p, new_wkv[1], new_shift[1], k_s, v_s)
```

```python
import functools

import jax
import jax.numpy as jnp
from jax import lax
from jax.experimental import pallas as pl
from jax.experimental.pallas import tpu as pltpu

F32 = jnp.float32
BF16 = jnp.bfloat16

HEAD = 64
N_KV_HEADS = 4
WINDOW = 128
CHUNK = 64
N_GROUPS = 4
EXPERTS_PER_GROUP = 8
N_EXPERTS = N_GROUPS * EXPERTS_PER_GROUP
TOP_K = 2
DEPTH = 2
ALPHA = (2.0 * DEPTH) ** 0.25
LN_EPS = 1e-5
GN_EPS = 64e-5
NEG_INF = -1e30
LOG_DECAY_SCALE = 0.6065306597126334

LANES = 128
PAIR = LANES // HEAD

PROMPT_ROWS = 512
MOE_ROWS = 256
MODS_TN = 1024
WKV_PASSES = 1
WKV_SLAB = 128
ATTN_HEADS_PER_STAGE = 32
DMA_UNROLL = 8
VMEM_LIMIT = 56 * 1024 * 1024

NN = (((1,), (0,)), ((), ()))
NT = (((1,), (1,)), ((), ()))
TN = (((0,), (0,)), ((), ()))


def _mm(a, b, dims=NN, passes=1):
    if passes == 6:
        return lax.dot_general(a, b, dims, precision=lax.Precision.HIGHEST, preferred_element_type=F32)
    ah = a.astype(BF16)
    bh = b.astype(BF16)
    out = lax.dot_general(ah, bh, dims, preferred_element_type=F32)
    if passes == 3:
        al = (a - ah.astype(F32)).astype(BF16)
        bl = (b - bh.astype(F32)).astype(BF16)
        out = out + lax.dot_general(ah, bl, dims, preferred_element_type=F32)
        out = out + lax.dot_general(al, bh, dims, preferred_element_type=F32)
    return out


def _bf16_terms(x, terms):
    out = []
    for _ in range(terms):
        t = x.astype(BF16)
        out.append(t)
        x = x - t.astype(F32)
    return out


def _mm_mask(a, b, dims=NN, split="lhs", terms=2):
    if split == "lhs":
        bb = b.astype(BF16)
        parts = [lax.dot_general(t, bb, dims, preferred_element_type=F32) for t in _bf16_terms(a, terms)]
    else:
        aa = a.astype(BF16)
        parts = [lax.dot_general(aa, t, dims, preferred_element_type=F32) for t in _bf16_terms(b, terms)]
    return functools.reduce(lambda p, q: p + q, parts)


def _sigmoid(x):
    return 1.0 / (1.0 + jnp.exp(-x))


def _layer_norm(v, g, b):
    mu = jnp.mean(v, axis=-1, keepdims=True)
    d = v - mu
    var = jnp.mean(d * d, axis=-1, keepdims=True)
    return d * lax.rsqrt(var + LN_EPS) * g + b


def _params(*sem):
    return pltpu.CompilerParams(dimension_semantics=sem, vmem_limit_bytes=VMEM_LIMIT)


def _mods_kernel(c_ref, w_ref, b_ref, o_ref):
    c = c_ref[...]
    a = (c * _sigmoid(c)).astype(BF16)
    o_ref[0] = jnp.dot(a, w_ref[0].astype(BF16), preferred_element_type=F32) + b_ref[0]


def _mods(c, w, b):
    g, d, n = w.shape
    m = c.shape[0]
    tn = min(MODS_TN, n)
    assert n % tn == 0
    return pl.pallas_call(
        _mods_kernel,
        out_shape=jax.ShapeDtypeStruct((g, m, n), F32),
        grid=(g, n // tn),
        in_specs=[
            pl.BlockSpec((m, d), lambda i, j: (0, 0)),
            pl.BlockSpec((1, d, tn), lambda i, j: (i, 0, j)),
            pl.BlockSpec((1, 1, tn), lambda i, j: (i, 0, j)),
        ],
        out_specs=pl.BlockSpec((1, m, tn), lambda i, j: (i, 0, j)),
        compiler_params=_params("arbitrary", "arbitrary"),
        name="mods",
    )(c, w, b)


def _modulate(x_ref, sc_ref, sh_ref):
    nb, tt, d = x_ref.shape
    h = x_ref[...] * (1.0 + sc_ref[...]) + sh_ref[...]
    return h.reshape(nb * tt, d)


def _token_shift(h, prev_scr, prev0_ref, tt, t_axis):
    rows, d = h.shape
    nb = rows // tt

    @pl.when(pl.program_id(t_axis) == 0)
    def _():
        prev_scr[...] = prev0_ref[...]

    prev = jnp.broadcast_to(prev_scr[...], (nb, tt, d)).reshape(rows, d)
    rolled = pltpu.roll(h, 1, 0)
    row = lax.broadcasted_iota(jnp.int32, (rows, 1), 0)
    first = (row & (tt - 1)) == 0
    hp = jnp.where(first, prev, rolled)
    prev_scr[...] = h.reshape(nb, tt, d)[:, tt - 1:tt, :]
    return hp


def _rkv_kernel(x_ref, sc_ref, sh_ref, prev0_ref, mu_ref, w_ref, o_ref, prev_scr):
    nb, tt, d = x_ref.shape
    h = _modulate(x_ref, sc_ref, sh_ref)
    hp = _token_shift(h, prev_scr, prev0_ref, tt, 2)
    xin = h + (hp - h) * mu_ref[0]
    out = jnp.dot(xin.astype(BF16), w_ref[0], preferred_element_type=F32)
    o_ref[0] = out.reshape(nb, tt, out.shape[-1])


def _rkv(x, sc, sh, prev0, mu3, w3, nb, tt):
    NB, TT, D = x.shape
    seq = lambda j, b, t: (b, t, 0)
    per = lambda j, b, t: (b, 0, 0)
    return pl.pallas_call(
        _rkv_kernel,
        out_shape=jax.ShapeDtypeStruct((3, NB, TT, D), F32),
        grid=(3, NB // nb, TT // tt),
        in_specs=[
            pl.BlockSpec((nb, tt, D), seq),
            pl.BlockSpec((nb, 1, D), per),
            pl.BlockSpec((nb, 1, D), per),
            pl.BlockSpec((nb, 1, D), per),
            pl.BlockSpec((1, 1, D), lambda j, b, t: (j, 0, 0)),
            pl.BlockSpec((1, D, D), lambda j, b, t: (j, 0, 0)),
        ],
        out_specs=pl.BlockSpec((1, nb, tt, D), lambda j, b, t: (j, b, t, 0)),
        scratch_shapes=[pltpu.VMEM((nb, 1, D), F32)],
        compiler_params=_params("arbitrary", "arbitrary", "arbitrary"),
        name="tmix_rkv",
    )(x, sc, sh, prev0, mu3, w3)


def _lora_kernel(x_ref, sc_ref, sh_ref, prev0_ref, mu_ref, w1_ref, w2_ref, w0_ref, a1_ref, a2_ref, a0_ref,
                 g1_ref, g2_ref, lw_ref, a_ref, g_ref, last_ref, prev_scr):
    nb, tt, d = x_ref.shape
    h = _modulate(x_ref, sc_ref, sh_ref)
    hp = _token_shift(h, prev_scr, prev0_ref, tt, 1)
    dx = hp - h
    last_ref[...] = h.reshape(nb, tt, d)[:, tt - 1:tt, :]

    xw = (h + dx * mu_ref[0:1, :]).astype(BF16)
    zw = jnp.tanh(jnp.dot(xw, w1_ref[...], preferred_element_type=F32))
    zw = jnp.dot(zw.astype(BF16), w2_ref[...], preferred_element_type=F32) + w0_ref[...]
    lw_ref[...] = (-LOG_DECAY_SCALE * _sigmoid(zw)).reshape(nb, tt, d)

    xa = (h + dx * mu_ref[1:2, :]).astype(BF16)
    za = jnp.dot(xa, a1_ref[...], preferred_element_type=F32)
    za = jnp.dot(za.astype(BF16), a2_ref[...], preferred_element_type=F32) + a0_ref[...]
    a_ref[...] = _sigmoid(za).reshape(nb, tt, d)

    xg = (h + dx * mu_ref[2:3, :]).astype(BF16)
    zg = _sigmoid(jnp.dot(xg, g1_ref[...], preferred_element_type=F32))
    g_ref[...] = jnp.dot(zg.astype(BF16), g2_ref[...], preferred_element_type=F32).reshape(nb, tt, d)


def _lora(x, sc, sh, prev0, mu3, w1, w2, w0, a1, a2, a0, g1, g2, nb, tt):
    NB, TT, D = x.shape
    seq = lambda b, t: (b, t, 0)
    per = lambda b, t: (b, 0, 0)
    full = lambda a: pl.BlockSpec(a.shape, lambda b, t: (0,) * a.ndim)
    big = jax.ShapeDtypeStruct((NB, TT, D), F32)
    return pl.pallas_call(
        _lora_kernel,
        out_shape=(big, big, big, jax.ShapeDtypeStruct((NB, 1, D), F32)),
        grid=(NB // nb, TT // tt),
        in_specs=[
            pl.BlockSpec((nb, tt, D), seq),
            pl.BlockSpec((nb, 1, D), per),
            pl.BlockSpec((nb, 1, D), per),
            pl.BlockSpec((nb, 1, D), per),
            full(mu3), full(w1), full(w2), full(w0), full(a1), full(a2), full(a0), full(g1), full(g2),
        ],
        out_specs=(pl.BlockSpec((nb, tt, D), seq),) * 3 + (pl.BlockSpec((nb, 1, D), per),),
        scratch_shapes=[pltpu.VMEM((nb, 1, D), F32)],
        compiler_params=_params("arbitrary", "arbitrary"),
        name="tmix_lora",
    )(x, sc, sh, prev0, mu3, w1, w2, w0, a1, a2, a0, g1, g2)


def _wkv_kernel(r_ref, k_ref, v_ref, lw_ref, a_ref, g_ref, kkc_ref, kac_ref, rk_ref, lng_ref, lnb_ref, s0_ref,
                z_ref, sfin_ref, s_scr, *, passes):
    L = r_ref.shape[2]
    n_slabs, W = s_scr.shape[:2]
    G = W // HEAD
    c = pl.program_id(1)

    @pl.when(c == 0)
    def _():
        s_scr[...] = jnp.zeros(s_scr.shape, F32)
        for h in range(n_slabs * G):
            o = (h % G) * HEAD
            s_scr[h // G, o:o + HEAD, o:o + HEAD] = s0_ref[0, h]

    lane = lax.broadcasted_iota(jnp.int32, (1, W), 1)
    head_lanes = [(lane >= j * HEAD) & (lane < (j + 1) * HEAD) for j in range(G)]
    col = lax.broadcasted_iota(jnp.int32, (1, G * L), 1)
    head_cols = [(col >= j * L) & (col < (j + 1) * L) for j in range(G)]
    ri = lax.broadcasted_iota(jnp.int32, (L, G * L), 0)
    ci = lax.broadcasted_iota(jnp.int32, (L, G * L), 1) & (L - 1)
    strict = ri > ci
    incl = ri >= ci
    tril = jnp.where(incl[:, :L], 1.0, 0.0).astype(F32)
    bi = lax.broadcasted_iota(jnp.int32, (W, W), 0) // HEAD
    bj = lax.broadcasted_iota(jnp.int32, (W, W), 1) // HEAD
    same_head = bi == bj
    head_ones = jnp.where(same_head, 1.0, 0.0).astype(F32)

    def by_head(t, sel):
        return jnp.concatenate([jnp.where(s, t, 0.0) for s in sel], axis=0)

    slabs = list(range(n_slabs))
    sls = [slice(q * W, (q + 1) * W) for q in slabs]
    s0 = [s_scr[q] for q in slabs]
    rp = [r_ref[0, 0, :, sl] for sl in sls]
    kp = [k_ref[0, 0, :, sl] for sl in sls]
    vp = [v_ref[0, 0, :, sl] for sl in sls]
    lwp = [lw_ref[0, :, sl] for sl in sls]
    ap = [a_ref[0, :, sl] for sl in sls]
    cum = [_mm_mask(tril, t, split="rhs", terms=3) for t in lwp]
    kk0 = [k * kkc_ref[:, sl] for k, sl in zip(kp, sls)]
    n2 = [_mm_mask(t * t, head_ones, terms=1) for t in kk0]
    kkp = [t * lax.rsqrt(jnp.maximum(n, 1e-24)) for t, n in zip(kk0, n2)]
    khp = [k * (1.0 + (a - 1.0) * kac_ref[:, sl]) for k, a, sl in zip(kp, ap, sls)]
    c_incl = [jnp.exp(t) for t in cum]
    c_inv = [jnp.exp(-t) for t in cum]
    kt = [kk * jnp.exp(t - lw) for kk, t, lw in zip(kkp, cum, lwp)]
    rt = [r * ci_ for r, ci_ in zip(rp, c_incl)]
    kti = [kh * cv for kh, cv in zip(khp, c_inv)]
    bti = [kk * a * cv for kk, a, cv in zip(kkp, ap, c_inv)]
    lhs = [jnp.concatenate([a_, b_], axis=0) for a_, b_ in zip(kt, rt)]

    gk = [_mm(l_, by_head(t, head_lanes), NT, passes) for l_, t in zip(lhs, kti)]
    gb = [_mm(l_, by_head(t, head_lanes), NT, passes) for l_, t in zip(lhs, bti)]
    p0 = [_mm(l_, s_, NT, passes) for l_, s_ in zip(lhs, s0)]
    a_kk = [jnp.where(strict, t[:L], 0.0) for t in gk]
    a_rk = [jnp.where(incl, t[L:], 0.0) for t in gk]
    a_rb = [jnp.where(incl, t[L:], 0.0) for t in gb]
    pw = [jnp.where(strict, -t[:L], 0.0) for t in gb]
    sv = [by_head(t, head_lanes) for t in vp]
    x = [p[:L] + _mm(a_, s_, NN, passes) for p, a_, s_ in zip(p0, a_kk, sv)]
    u = x
    n = 1
    while n < L:
        if n * 2 >= L:
            u = [u_ + _mm(p_, by_head(u_, head_lanes), NN, passes) for p_, u_ in zip(pw, u)]
        elif (G * L) % LANES == 0:
            both = [_mm(p_, jnp.concatenate([by_head(p_, head_cols), by_head(u_, head_lanes)], axis=1), NN, passes)
                    for p_, u_ in zip(pw, u)]
            pw = [t[:, :G * L] for t in both]
            u = [u_ + t[:, G * L:] for u_, t in zip(u, both)]
        else:
            u = [u_ + _mm(p_, by_head(u_, head_lanes), NN, passes) for p_, u_ in zip(pw, u)]
            pw = [_mm(p_, by_head(p_, head_cols), NN, passes) for p_ in pw]
        n *= 2
    y = [p[L:] + _mm(ak, s_, NN, passes) - _mm(ab, by_head(u_, head_lanes), NN, passes)
         for p, ak, s_, ab, u_ in zip(p0, a_rk, sv, a_rb, u)]
    ds = [_mm(jnp.concatenate([v_, u_], axis=0), jnp.concatenate([kt_, -bt_], axis=0), TN, passes)
          for v_, u_, kt_, bt_ in zip(vp, u, kti, bti)]
    s_new = [(s_ + jnp.where(same_head, d_, 0.0)) * ci_[L - 1:L, :] for s_, d_, ci_ in zip(s0, ds, c_incl)]

    mean = [_mm_mask(t, head_ones, terms=1) * (1.0 / HEAD) for t in y]
    dlt = [t - m_ for t, m_ in zip(y, mean)]
    var = [_mm_mask(t * t, head_ones, terms=1) * (1.0 / HEAD) for t in dlt]
    bonus = [_mm_mask(r * kh * rk_ref[:, sl], head_ones, terms=1) * v for r, kh, v, sl in zip(rp, khp, vp, sls)]
    for q, sl in zip(slabs, sls):
        yn = dlt[q] * lax.rsqrt(var[q] + GN_EPS) * lng_ref[:, sl] + lnb_ref[:, sl]
        z_ref[0, :, sl] = ((yn + bonus[q]) * g_ref[0, :, sl]).astype(z_ref.dtype)
    for q in slabs:
        s_scr[q] = s_new[q]

    @pl.when(c == pl.num_programs(1) - 1)
    def _():
        for h in range(n_slabs * G):
            o = (h % G) * HEAD
            sfin_ref[0, h] = s_scr[h // G, o:o + HEAD, o:o + HEAD]


def _wkv(rkv, lw, a, g, kkc, kac, rk, lng, lnb, s0, L, passes=WKV_PASSES):
    _, NB, TT, D = rkv.shape
    H = D // HEAD
    seq = lambda b, t: (b, t, 0)
    vec = pl.BlockSpec((1, D), lambda b, t: (0, 0))
    st = pl.BlockSpec((1, H, HEAD, HEAD), lambda b, t: (b, 0, 0, 0))
    return pl.pallas_call(
        functools.partial(_wkv_kernel, passes=passes),
        out_shape=(jax.ShapeDtypeStruct((NB, TT, D), BF16), jax.ShapeDtypeStruct((NB, H, HEAD, HEAD), F32)),
        grid=(NB, TT // L),
        in_specs=[
            pl.BlockSpec((1, 1, L, D), lambda b, t: (0, b, t, 0)),
            pl.BlockSpec((1, 1, L, D), lambda b, t: (1, b, t, 0)),
            pl.BlockSpec((1, 1, L, D), lambda b, t: (2, b, t, 0)),
            pl.BlockSpec((1, L, D), seq),
            pl.BlockSpec((1, L, D), seq),
            pl.BlockSpec((1, L, D), seq),
            vec, vec, vec, vec, vec, st,
        ],
        out_specs=(pl.BlockSpec((1, L, D), seq), st),
        scratch_shapes=[pltpu.VMEM((H * HEAD // WKV_SLAB, WKV_SLAB, WKV_SLAB), F32)],
        compiler_params=_params("arbitrary", "arbitrary"),
        name="wkv",
    )(rkv, rkv, rkv, lw, a, g, kkc, kac, rk, lng, lnb, s0)


def _outproj_ln_kernel(a_ref, x_ref, gate_ref, lng_ref, lnb_ref, w_ref, o_ref):
    nb, tt, d = x_ref.shape
    a = a_ref[...].reshape(nb * tt, a_ref.shape[-1])
    y = jnp.dot(a, w_ref[...], preferred_element_type=F32).reshape(nb, tt, d)
    v = ALPHA * x_ref[...] + (1.0 + gate_ref[...]) * y
    o_ref[...] = _layer_norm(v, lng_ref[...], lnb_ref[...])


def _outproj_ln(a, x, gate, lng, lnb, w, nb, tt):
    NB, TT, D = x.shape
    seq = lambda b, t: (b, t, 0)
    per = lambda b, t: (b, 0, 0)
    vec = pl.BlockSpec((1, 1, D), lambda b, t: (0, 0, 0))
    return pl.pallas_call(
        _outproj_ln_kernel,
        out_shape=jax.ShapeDtypeStruct((NB, TT, D), F32),
        grid=(NB // nb, TT // tt),
        in_specs=[
            pl.BlockSpec((nb, tt, a.shape[-1]), seq),
            pl.BlockSpec((nb, tt, D), seq),
            pl.BlockSpec((nb, 1, D), per),
            vec, vec,
            pl.BlockSpec(w.shape, lambda b, t: (0, 0)),
        ],
        out_specs=pl.BlockSpec((nb, tt, D), seq),
        compiler_params=_params("arbitrary", "arbitrary"),
        name="outproj_ln",
    )(a, x, gate, lng, lnb, w)


def _modproj_kernel(x_ref, sc_ref, sh_ref, *refs):
    n = len(refs) // 2
    nb, tt, d = x_ref.shape
    h = _modulate(x_ref, sc_ref, sh_ref).astype(BF16)
    for w_ref, o_ref in zip(refs[:n], refs[n:]):
        out = jnp.dot(h, w_ref[...], preferred_element_type=F32)
        o_ref[...] = out.reshape(nb, tt, out.shape[-1]).astype(o_ref.dtype)


def _modproj(x, sc, sh, ws, nb, tt, out_dtypes):
    NB, TT, D = x.shape
    seq = lambda b, t: (b, t, 0)
    per = lambda b, t: (b, 0, 0)
    return pl.pallas_call(
        _modproj_kernel,
        out_shape=tuple(jax.ShapeDtypeStruct((NB, TT, w.shape[1]), dt) for w, dt in zip(ws, out_dtypes)),
        grid=(NB // nb, TT // tt),
        in_specs=[
            pl.BlockSpec((nb, tt, D), seq),
            pl.BlockSpec((nb, 1, D), per),
            pl.BlockSpec((nb, 1, D), per),
        ] + [pl.BlockSpec(w.shape, lambda b, t: (0, 0)) for w in ws],
        out_specs=tuple(pl.BlockSpec((nb, tt, w.shape[1]), seq) for w in ws),
        compiler_params=_params("arbitrary", "arbitrary"),
        name="modproj",
    )(x, sc, sh, *ws)


ROUTE_E, ROUTE_RANK, ROUTE_GATE = 0, 2, 4
BIG_NEG = -3.0e38


def _route_kernel(x_ref, sc_ref, sh_ref, w_ref, b_ref, cnt0_ref, route_ref, cnt_ref, cnt_scr):
    first = (pl.program_id(0) == 0) & (pl.program_id(1) == 0)

    @pl.when(first)
    def _():
        cnt_scr[...] = cnt0_ref[...]

    h = _modulate(x_ref, sc_ref, sh_ref)
    rows = h.shape[0]
    lg = _mm(h, w_ref[...], NN, passes=3) + b_ref[...]
    lane = lax.broadcasted_iota(jnp.int32, (1, LANES), 1)

    def first_argmax(v, vmax):
        return jnp.min(jnp.where(v == vmax, lane, LANES), axis=-1, keepdims=True)

    is_g = lane < N_GROUPS
    gl = jnp.where(is_g, lg, BIG_NEG)
    gmax = jnp.max(gl, axis=-1, keepdims=True)
    g_sel = first_argmax(gl, gmax)
    g_prob = 1.0 / jnp.sum(jnp.where(is_g, jnp.exp(lg - gmax), 0.0), axis=-1, keepdims=True)
    lo = N_GROUPS + EXPERTS_PER_GROUP * g_sel
    el = jnp.where((lane >= lo) & (lane < lo + EXPERTS_PER_GROUP), lg, BIG_NEG)
    v1 = jnp.max(el, axis=-1, keepdims=True)
    i1 = first_argmax(el, v1)
    el2 = jnp.where(lane == i1, BIG_NEG, el)
    v2 = jnp.max(el2, axis=-1, keepdims=True)
    i2 = first_argmax(el2, v2)
    e1 = i1 - N_GROUPS
    e2 = i2 - N_GROUPS
    t = jnp.exp(v2 - v1)
    p1 = 1.0 / (1.0 + t)
    gate1 = p1 * g_prob
    gate2 = t * p1 * g_prob

    oh1 = jnp.where(lane == e1, 1.0, 0.0)
    oh2 = jnp.where(lane == e2, 1.0, 0.0)
    ri = lax.broadcasted_iota(jnp.int32, (rows, rows), 0)
    ci = lax.broadcasted_iota(jnp.int32, (rows, rows), 1)
    before = jnp.where(ri > ci, 1.0, 0.0).astype(BF16)
    c1 = jnp.dot(before, oh1.astype(BF16), preferred_element_type=F32)
    c2 = jnp.dot(before, oh2.astype(BF16), preferred_element_type=F32)
    tot1 = jnp.sum(oh1, axis=0, keepdims=True)
    tot2 = jnp.sum(oh2, axis=0, keepdims=True)
    base = cnt_scr[...]
    rank1 = jnp.sum(oh1 * (base + c1), axis=-1, keepdims=True)
    rank2 = jnp.sum(oh2 * (base + tot1 + c2), axis=-1, keepdims=True)
    cnt_new = base + tot1 + tot2
    cnt_scr[...] = cnt_new
    cnt_ref[...] = cnt_new

    out = jnp.where(lane == ROUTE_E, e1.astype(F32), 0.0)
    out = jnp.where(lane == ROUTE_E + 1, e2.astype(F32), out)
    out = jnp.where(lane == ROUTE_RANK, rank1, out)
    out = jnp.where(lane == ROUTE_RANK + 1, rank2, out)
    out = jnp.where(lane == ROUTE_GATE, gate1, out)
    out = jnp.where(lane == ROUTE_GATE + 1, gate2, out)
    route_ref[...] = out


def _route(x, sc, sh, w, b, cnt0, nb, tt):
    NB, TT, D = x.shape
    steps_t = TT // tt
    seq = lambda b_, t: (b_, t, 0)
    per = lambda b_, t: (b_, 0, 0)
    one = lambda b_, t: (0, 0)
    return pl.pallas_call(
        _route_kernel,
        out_shape=(jax.ShapeDtypeStruct((NB * TT, LANES), F32), jax.ShapeDtypeStruct((1, LANES), F32)),
        grid=(NB // nb, steps_t),
        in_specs=[
            pl.BlockSpec((nb, tt, D), seq),
            pl.BlockSpec((nb, 1, D), per),
            pl.BlockSpec((nb, 1, D), per),
            pl.BlockSpec(w.shape, one),
            pl.BlockSpec(b.shape, one),
            pl.BlockSpec((1, LANES), one),
        ],
        out_specs=(pl.BlockSpec((nb * tt, LANES), lambda b_, t: (b_ * steps_t + t, 0)),
                   pl.BlockSpec((1, LANES), one)),
        scratch_shapes=[pltpu.VMEM((1, LANES), F32)],
        compiler_params=_params("arbitrary", "arbitrary"),
        name="route",
    )(x, sc, sh, w, b, cnt0)


def _row_copy(src, dst, sem):
    return pltpu.make_async_copy(src, dst, sem)


def _dispatch_kernel(dest_ref, x_ref, sc_ref, sh_ref, xb_in_ref, xb_ref, buf, sems):
    del xb_in_ref
    steps_t = pl.num_programs(1)
    i = pl.program_id(0) * steps_t + pl.program_id(1)
    n_steps = pl.num_programs(0) * steps_t
    slot = i % 2
    rows, half = buf.shape[1:]

    def wait_slot(s):
        for _ in range(TOP_K):
            _row_copy(buf.at[s], xb_ref.at[pl.ds(0, rows)], sems.at[s]).wait()

    @pl.when(i >= 2)
    def _():
        wait_slot(slot)

    h = _modulate(x_ref, sc_ref, sh_ref)
    bits = pltpu.bitcast(h.astype(BF16).astype(F32), jnp.uint32)
    buf[slot] = bits[:, :half] | (bits[:, half:] >> 16)

    def issue(r, carry):
        for j in range(TOP_K):
            d = dest_ref[TOP_K * r + j]
            _row_copy(buf.at[slot, pl.ds(r, 1)], xb_ref.at[pl.ds(d, 1)], sems.at[slot]).start()
        return carry
    lax.fori_loop(0, rows, issue, 0, unroll=DMA_UNROLL)

    @pl.when(i == n_steps - 1)
    def _():
        wait_slot(slot)

    @pl.when((i == n_steps - 1) & (i >= 1))
    def _():
        wait_slot(1 - slot)


def _moe_dispatch(dest_flat, x, sc, sh, xb, nb, tt):
    NB, TT, D = x.shape
    steps_t = TT // tt
    rows = nb * tt
    seq = lambda b_, t: (b_, t, 0)
    per = lambda b_, t: (b_, 0, 0)
    return pl.pallas_call(
        _dispatch_kernel,
        out_shape=jax.ShapeDtypeStruct(xb.shape, xb.dtype),
        grid=(NB // nb, steps_t),
        in_specs=[
            pl.BlockSpec((rows * TOP_K,), lambda b_, t: (b_ * steps_t + t,), memory_space=pltpu.SMEM),
            pl.BlockSpec((nb, tt, D), seq),
            pl.BlockSpec((nb, 1, D), per),
            pl.BlockSpec((nb, 1, D), per),
            pl.BlockSpec(memory_space=pl.ANY),
        ],
        out_specs=pl.BlockSpec(memory_space=pl.ANY),
        scratch_shapes=[pltpu.VMEM((2, rows, D // 2), jnp.uint32), pltpu.SemaphoreType.DMA((2,))],
        input_output_aliases={4: 0},
        compiler_params=_params("arbitrary", "arbitrary"),
        name="moe_dispatch",
    )(dest_flat, x, sc, sh, xb)


def _moe_kernel(be_ref, used_ref, x_ref, w1_ref, w3_ref, w2_ref, o_ref, w1s, w3s, w2s):
    i = pl.program_id(0)
    e = be_ref[i]
    e_prev = be_ref[jnp.maximum(i - 1, 0)]

    @pl.when((i == 0) | (e != e_prev))
    def _():
        w1s[...] = w1_ref[0, 0].astype(BF16)
        w3s[...] = w3_ref[0, 0].astype(BF16)
        w2s[...] = w2_ref[0, 0].astype(BF16)

    @pl.when(i < used_ref[0])
    def _():
        p = x_ref[...]
        hi = pltpu.bitcast(p & jnp.uint32(0xFFFF0000), F32)
        lo = pltpu.bitcast(p << 16, F32)
        x = jnp.concatenate([hi, lo], axis=1).astype(BF16)
        h1 = jnp.dot(x, w1s[...], preferred_element_type=F32)
        h3 = jnp.dot(x, w3s[...], preferred_element_type=F32)
        act = (h1 * _sigmoid(h1) * h3).astype(BF16)
        o_ref[...] = jnp.dot(act, w2s[...], preferred_element_type=F32)

    @pl.when(i >= used_ref[0])
    def _():
        o_ref[...] = jnp.zeros(o_ref.shape, F32)


def _moe_ffn(blk_expert, n_used, xb, w1, w3, w2, layer):
    rows, half = xb.shape
    _, E, D, DE = w1.shape
    n_blocks = rows // MOE_ROWS
    return pl.pallas_call(
        _moe_kernel,
        out_shape=jax.ShapeDtypeStruct((rows, D), F32),
        grid_spec=pltpu.PrefetchScalarGridSpec(
            num_scalar_prefetch=2,
            grid=(n_blocks,),
            in_specs=[
                pl.BlockSpec((MOE_ROWS, half), lambda i, be, nu: (i, 0)),
                pl.BlockSpec((1, 1, D, DE), lambda i, be, nu: (layer, be[i], 0, 0)),
                pl.BlockSpec((1, 1, D, DE), lambda i, be, nu: (layer, be[i], 0, 0)),
                pl.BlockSpec((1, 1, DE, D), lambda i, be, nu: (layer, be[i], 0, 0)),
            ],
            out_specs=pl.BlockSpec((MOE_ROWS, D), lambda i, be, nu: (i, 0)),
            scratch_shapes=[pltpu.VMEM((D, DE), BF16), pltpu.VMEM((D, DE), BF16), pltpu.VMEM((DE, D), BF16)],
        ),
        compiler_params=_params("arbitrary"),
        name="moe_ffn",
    )(blk_expert, n_used, xb, w1, w3, w2)


def _combine_ln_kernel(dest_ref, dest_next_ref, x_ref, route_ref, gate_ref, lng_ref, lnb_ref, yb_ref, o_ref,
                       ybuf, sems):
    steps_t = pl.num_programs(1)
    i = pl.program_id(0) * steps_t + pl.program_id(1)
    n_steps = pl.num_programs(0) * steps_t
    slot = i % 2
    nb, tt, d = x_ref.shape
    rows = nb * tt

    def fetch(d_ref, s):
        def body(r, carry):
            for j in range(TOP_K):
                row = d_ref[TOP_K * r + j]
                _row_copy(yb_ref.at[pl.ds(row, 1)], ybuf.at[s, j, pl.ds(r, 1)], sems.at[s]).start()
            return carry
        lax.fori_loop(0, rows, body, 0, unroll=DMA_UNROLL)

    @pl.when(i == 0)
    def _():
        fetch(dest_ref, slot)

    for j in range(TOP_K):
        _row_copy(yb_ref.at[pl.ds(0, rows)], ybuf.at[slot, j], sems.at[slot]).wait()

    @pl.when(i + 1 < n_steps)
    def _():
        fetch(dest_next_ref, 1 - slot)

    rt = route_ref[...]
    y = rt[:, ROUTE_GATE:ROUTE_GATE + 1] * ybuf[slot, 0] + rt[:, ROUTE_GATE + 1:ROUTE_GATE + 2] * ybuf[slot, 1]
    v = ALPHA * x_ref[...] + (1.0 + gate_ref[...]) * y.reshape(nb, tt, d)
    o_ref[...] = _layer_norm(v, lng_ref[...], lnb_ref[...])


def _combine_ln(dest_flat, x, route, gate, lng, lnb, yb, nb, tt, row0):
    NB, TT, D = x.shape
    steps_t = TT // tt
    n_steps = (NB // nb) * steps_t
    rows = nb * tt
    blk0 = row0 // rows
    seq = lambda b_, t: (b_, t, 0)
    per = lambda b_, t: (b_, 0, 0)
    vec = pl.BlockSpec((1, 1, D), lambda b_, t: (0, 0, 0))
    cur = lambda b_, t: (blk0 + b_ * steps_t + t,)
    nxt = lambda b_, t: (blk0 + jnp.minimum(b_ * steps_t + t + 1, n_steps - 1),)
    return pl.pallas_call(
        _combine_ln_kernel,
        out_shape=jax.ShapeDtypeStruct((NB, TT, D), F32),
        grid=(NB // nb, steps_t),
        in_specs=[
            pl.BlockSpec((rows * TOP_K,), cur, memory_space=pltpu.SMEM),
            pl.BlockSpec((rows * TOP_K,), nxt, memory_space=pltpu.SMEM),
            pl.BlockSpec((nb, tt, D), seq),
            pl.BlockSpec((rows, LANES), lambda b_, t: (blk0 + b_ * steps_t + t, 0)),
            pl.BlockSpec((nb, 1, D), per),
            vec, vec,
            pl.BlockSpec(memory_space=pl.ANY),
        ],
        out_specs=pl.BlockSpec((nb, tt, D), seq),
        scratch_shapes=[pltpu.VMEM((2, TOP_K, rows, D), F32), pltpu.SemaphoreType.DMA((2,))],
        compiler_params=_params("arbitrary", "arbitrary"),
        name="moe_combine_ln",
    )(dest_flat, dest_flat, x, route, gate, lng, lnb, yb)


def _moe_sublayer(xs, mods, gates, lng, lnb, w_router, b_router, w1, w3, w2, layer, cfgs):
    D = xs[0].shape[-1]
    routes = []
    cnt = jnp.zeros((1, LANES), F32)
    for x, (sh, sc), (nb, tt) in zip(xs, mods, cfgs):
        route, cnt = _route(x, sc, sh, w_router, b_router, cnt, nb, tt)
        routes.append(route)
    route = jnp.concatenate(routes, axis=0)
    n = route.shape[0]
    counts = cnt[0, :N_EXPERTS].astype(jnp.int32)
    padded = (counts + MOE_ROWS - 1) // MOE_ROWS * MOE_ROWS
    pend = jnp.cumsum(padded)
    pstart = pend - padded
    eid = route[:, ROUTE_E:ROUTE_E + TOP_K].astype(jnp.int32)
    rank = route[:, ROUTE_RANK:ROUTE_RANK + TOP_K].astype(jnp.int32)
    experts = jnp.arange(N_EXPERTS, dtype=jnp.int32)
    dest = (jnp.sum(jnp.where(eid[..., None] == experts, pstart, 0), axis=-1) + rank).reshape(n * TOP_K)
    n_blocks = -(-n * TOP_K // MOE_ROWS) + N_EXPERTS
    blk_start = jnp.arange(n_blocks, dtype=jnp.int32) * MOE_ROWS
    blk_expert = jnp.minimum(jnp.sum((blk_start[:, None] >= pend[None, :]).astype(jnp.int32), axis=1), N_EXPERTS - 1)
    n_used = (pend[-1] // MOE_ROWS).astype(jnp.int32).reshape(1)

    xb = jnp.zeros((n_blocks * MOE_ROWS, D // 2), jnp.uint32)
    row0s = []
    row = 0
    for x, (sh, sc), (nb, tt) in zip(xs, mods, cfgs):
        rows_p = x.shape[0] * x.shape[1]
        xb = _moe_dispatch(lax.dynamic_slice_in_dim(dest, row * TOP_K, rows_p * TOP_K), x, sc, sh, xb, nb, tt)
        row0s.append(row)
        row += rows_p
    yb = _moe_ffn(blk_expert, n_used, xb, w1, w3, w2, layer)
    return [_combine_ln(dest, x, route, gt, lng, lnb, yb, nb, tt, row0)
            for x, gt, (nb, tt), row0 in zip(xs, gates, cfgs, row0s)]


def _attn_kernel(sinks_ref, q_ref, *refs, n_band, first_key_chunk):
    k_refs = refs[:n_band]
    v_refs = refs[n_band:2 * n_band]
    o_ref = refs[2 * n_band]
    tq = q_ref.shape[1]
    kd = jnp.concatenate([r[0] for r in k_refs], axis=0) if n_band > 1 else k_refs[0][0]
    vd = jnp.concatenate([r[0] for r in v_refs], axis=0) if n_band > 1 else v_refs[0][0]
    nk = kd.shape[0]
    qi = lax.broadcasted_iota(jnp.int32, (tq, nk), 0)
    kj = lax.broadcasted_iota(jnp.int32, (tq, nk), 1)
    neg_dist = -jnp.abs((nk - tq) + qi - kj).astype(F32)
    if first_key_chunk is not None:
        k_pos = (pl.program_id(1) + first_key_chunk) * tq + kj
        neg_dist = jnp.where(k_pos >= 0, neg_dist, NEG_INF * 2.0 ** 8)
    lane = lax.broadcasted_iota(jnp.int32, (1, LANES), 1)
    n_q_heads = q_ref.shape[2] // HEAD
    q_per_kv = n_q_heads // N_KV_HEADS
    scale = HEAD ** -0.5
    k_half, v_half = [], []
    for g in range(N_KV_HEADS):
        kg = kd[:, g * LANES:(g + 1) * LANES]
        vg = vd[:, g * LANES:(g + 1) * LANES]
        zero = jnp.zeros_like(kg)
        k_half.append([jnp.where(lane < HEAD, kg, zero), jnp.where(lane >= HEAD, kg, zero)])
        v_half.append([jnp.where(lane < HEAD, vg, zero), jnp.where(lane >= HEAD, vg, zero)])
    per_stage = min(ATTN_HEADS_PER_STAGE, n_q_heads)
    for h0 in range(0, n_q_heads, per_stage):
        hs = list(range(h0, h0 + per_stage))
        qp = {h // PAIR: (q_ref[0, :, (h // PAIR) * LANES:(h // PAIR + 1) * LANES] * scale).astype(BF16) for h in hs}
        s = [lax.dot_general(qp[h // PAIR], k_half[h // q_per_kv][h % PAIR], NT, preferred_element_type=F32)
             for h in hs]
        s = [t + (2.0 ** (-8.0 * (h + 1) / n_q_heads)) * neg_dist for t, h in zip(s, hs)]
        mx = [jnp.maximum(jnp.max(t, axis=-1, keepdims=True), sinks_ref[h]) for t, h in zip(s, hs)]
        pr = [jnp.exp(t - m) for t, m in zip(s, mx)]
        den = [jnp.sum(t, axis=-1, keepdims=True) + jnp.exp(sinks_ref[h] - m) for t, m, h in zip(pr, mx, hs)]
        pr = [(t * (1.0 / d)).astype(BF16) for t, d in zip(pr, den)]
        o = [jnp.dot(t, v_half[h // q_per_kv][h % PAIR], preferred_element_type=F32) for t, h in zip(pr, hs)]
        for j in range(0, len(hs), PAIR):
            pair = hs[j] // PAIR
            o_ref[0, :, pair * LANES:(pair + 1) * LANES] = (o[j] + o[j + 1]).astype(o_ref.dtype)


def _attn_prompt(sinks, q, kd, vd):
    B, T, D = q.shape
    nc = T // CHUNK
    w_chunks = WINDOW // CHUNK
    n_band = w_chunks + 1
    KW = kd.shape[-1]
    band = [pl.BlockSpec((1, CHUNK, KW), (lambda b, c, s, j=j: (b, jnp.maximum(c - w_chunks + j, 0), 0)))
            for j in range(n_band)]
    return pl.pallas_call(
        functools.partial(_attn_kernel, n_band=n_band, first_key_chunk=-w_chunks),
        out_shape=jax.ShapeDtypeStruct((B, T, D), BF16),
        grid_spec=pltpu.PrefetchScalarGridSpec(
            num_scalar_prefetch=1,
            grid=(B, nc),
            in_specs=[pl.BlockSpec((1, CHUNK, D), lambda b, c, s: (b, c, 0))] + band + band,
            out_specs=pl.BlockSpec((1, CHUNK, D), lambda b, c, s: (b, c, 0)),
        ),
        compiler_params=_params("arbitrary", "arbitrary"),
        name="attn_prompt",
    )(sinks, q, *([kd] * n_band), *([vd] * n_band))


def _attn_sample(sinks, q, kd, vd):
    B, T, D = q.shape
    NK, KW = kd.shape[1:]
    return pl.pallas_call(
        functools.partial(_attn_kernel, n_band=1, first_key_chunk=None),
        out_shape=jax.ShapeDtypeStruct((B, T, D), BF16),
        grid_spec=pltpu.PrefetchScalarGridSpec(
            num_scalar_prefetch=1,
            grid=(B, 1),
            in_specs=[pl.BlockSpec((1, T, D), lambda b, c, s: (b, 0, 0)),
                      pl.BlockSpec((1, NK, KW), lambda b, c, s: (b, 0, 0)),
                      pl.BlockSpec((1, NK, KW), lambda b, c, s: (b, 0, 0))],
            out_specs=pl.BlockSpec((1, T, D), lambda b, c, s: (b, 0, 0)),
        ),
        compiler_params=_params("arbitrary", "arbitrary"),
        name="attn_sample",
    )(sinks, q, kd, vd)


def _dup_heads(t):
    B, T, _ = t.shape
    t4 = t.reshape(B, T, N_KV_HEADS, 1, HEAD)
    return jnp.broadcast_to(t4, (B, T, N_KV_HEADS, PAIR, HEAD)).reshape(B, T, N_KV_HEADS * LANES).astype(BF16)


def _pad_cols(w, n):
    return jnp.pad(w, ((0, 0), (0, n - w.shape[1])))


def _pad_rows(w, n):
    return jnp.pad(w, ((0, n - w.shape[0]), (0, 0)))


def kernel(x_prompt, x_sample, c_prompt, c_sample, state_wkv, state_shift, cache_k, cache_v, mod_w, mod_b, ln_g, ln_b, rw_mu, rw_wr, rw_wk, rw_wv, rw_w0, rw_w1, rw_w2, rw_a0, rw_a1, rw_a2, rw_g1, rw_g2, rw_kk, rw_ka, rw_rk, rw_lnx_g, rw_lnx_b, rw_wo, kv_mod_w, kv_mod_b, w_kv, at_wq, at_sinks, at_wo, moe_wg, moe_bg, moe_wr, moe_br, moe_w1, moe_w3, moe_w2):
    B, T, D = x_prompt.shape
    BS, TS, _ = x_sample.shape
    H = D // HEAD
    KVW = N_KV_HEADS * HEAD
    cfgs = [(1, min(PROMPT_ROWS, T)), (BS, TS)]
    chunk_len = [CHUNK, TS]

    c_all = jnp.concatenate([c_prompt, c_sample], axis=0)
    m_all = _mods(c_all, mod_w.reshape(DEPTH * 2, D, 3 * D), mod_b.reshape(DEPTH * 2, 1, 3 * D))
    kvm = _mods(c_all, kv_mod_w[None], kv_mod_b[None, None])[0]
    rows = [slice(0, B), slice(B, B + BS)]

    def mod(layer, sub, part, path):
        return m_all[layer * 2 + sub, rows[path], part * D:(part + 1) * D][:, None, :]

    xs = [x_prompt, x_sample]
    shift0 = [jnp.zeros((B, 1, D), F32), state_shift[0][:, None, :]]
    wkv0 = [jnp.zeros((B, H, HEAD, HEAD), F32), state_wkv[0]]

    mu = rw_mu[0]
    mu_rkv = jnp.stack([mu[0], mu[2], mu[3]])[:, None, :]
    mu_lora = jnp.stack([mu[1], mu[4], mu[5]])
    w_rkv = jnp.stack([rw_wr[0], rw_wk[0], rw_wv[0]]).astype(BF16)
    lw1 = _pad_cols(rw_w1[0], LANES).astype(BF16)
    lw2 = _pad_rows(rw_w2[0], LANES).astype(BF16)
    la1 = _pad_cols(rw_a1[0], LANES).astype(BF16)
    la2 = _pad_rows(rw_a2[0], LANES).astype(BF16)
    lg1 = rw_g1[0].astype(BF16)
    lg2 = rw_g2[0].astype(BF16)
    wo = rw_wo[0].astype(BF16)
    vec = lambda v: v.reshape(1, D)
    new_wkv, new_shift, x1 = [], [], []
    for p in range(2):
        nb, tt = cfgs[p]
        sh, sc, gt = mod(0, 0, 0, p), mod(0, 0, 1, p), mod(0, 0, 2, p)
        rkv = _rkv(xs[p], sc, sh, shift0[p], mu_rkv, w_rkv, nb, tt)
        lw, a, g, last = _lora(xs[p], sc, sh, shift0[p], mu_lora, lw1, lw2, vec(rw_w0[0]), la1, la2, vec(rw_a0[0]),
                               lg1, lg2, nb, tt)
        z, s_fin = _wkv(rkv, lw, a, g, vec(rw_kk[0]), vec(rw_ka[0]), vec(rw_rk[0]), vec(rw_lnx_g[0]),
                        vec(rw_lnx_b[0]), wkv0[p], chunk_len[p])
        x1.append(_outproj_ln(z, xs[p], gt, ln_g[0, 0].reshape(1, 1, D), ln_b[0, 0].reshape(1, 1, D), wo, nb, tt))
        new_wkv.append(s_fin[None])
        new_shift.append(last.reshape(1, -1, D))

    def moe(layer, xin):
        w_router = _pad_cols(jnp.concatenate([moe_wg[layer], moe_wr[layer]], axis=1), LANES)
        b_router = _pad_cols(jnp.concatenate([moe_bg[layer], moe_br[layer]])[None, :], LANES)
        mods = [(mod(layer, 1, 0, p), mod(layer, 1, 1, p)) for p in range(2)]
        gates = [mod(layer, 1, 2, p) for p in range(2)]
        return _moe_sublayer(xin, mods, gates, ln_g[layer, 1].reshape(1, 1, D), ln_b[layer, 1].reshape(1, 1, D),
                             w_router, b_router, moe_w1, moe_w3, moe_w2, layer, cfgs)

    x2 = moe(0, x1)

    w_kv_b = w_kv.astype(BF16)
    w_kd, w_vd = (_dup_heads(w_kv_b[None, :, i * KVW:(i + 1) * KVW])[0] for i in range(2))
    kv = []
    for p in range(2):
        nb, tt = cfgs[p]
        kv_sh = kvm[rows[p], :D][:, None, :]
        kv_sc = kvm[rows[p], D:][:, None, :]
        kv.append(_modproj(x2[p], kv_sc, kv_sh, (w_kv_b, w_kd, w_vd), nb, tt, (F32, BF16, BF16)))
    k_sh = [t[0][..., :KVW] for t in kv]
    v_sh = [t[0][..., KVW:] for t in kv]

    wq = at_wq[0].astype(BF16)
    wo1 = at_wo[0].astype(BF16)
    sinks = at_sinks[0].astype(F32)
    x3 = []
    for p in range(2):
        nb, tt = cfgs[p]
        sh, sc, gt = mod(1, 0, 0, p), mod(1, 0, 1, p), mod(1, 0, 2, p)
        q, = _modproj(x2[p], sc, sh, (wq,), nb, tt, (F32,))
        if p == 0:
            o = _attn_prompt(sinks, q, kv[p][1], kv[p][2])
        else:
            k_all = jnp.concatenate([cache_k.reshape(BS, WINDOW, KVW), k_sh[p]], axis=1)
            v_all = jnp.concatenate([cache_v.reshape(BS, WINDOW, KVW), v_sh[p]], axis=1)
            o = _attn_sample(sinks, q, _dup_heads(k_all), _dup_heads(v_all))
        x3.append(_outproj_ln(o, x2[p], gt, ln_g[1, 0].reshape(1, 1, D), ln_b[1, 0].reshape(1, 1, D), wo1, nb, tt))

    x4 = moe(1, x3)

    k_p = k_sh[0][:, -WINDOW:].reshape(B, WINDOW, N_KV_HEADS, HEAD)
    v_p = v_sh[0][:, -WINDOW:].reshape(B, WINDOW, N_KV_HEADS, HEAD)
    k_s = k_sh[1].reshape(BS, TS, N_KV_HEADS, HEAD)
    v_s = v_sh[1].reshape(BS, TS, N_KV_HEADS, HEAD)
    return (x4[0], x4[1], new_wkv[0], new_shift[0], k_p, v_p, new_wkv[1], new_shift[1], k_s, v_s)
```
